```python
import math
import jax
import jax.numpy as jnp
from jax import lax
import numpy as np

D_MODEL = 4096
BATCH = 4
SEQ = 4096
DEPTH = 4

CTX_LEN = 256
GRID_W = 64
D_MIX = D_MODEL
HEAD_DIM = 128
ROPE_THETA = 10000.0
Q_BLOCK = 128
DA_HEADS = D_MIX // (8 * HEAD_DIM)
DA_V = 2 * HEAD_DIM
GQ_HEADS = D_MIX // (4 * HEAD_DIM)
GQ_KV_HEADS = GQ_HEADS // 4
HY_WIDTH = D_MIX // 4
HY_ORDER = 2
HY_SHORT = 3
HY_EMB = 33
HY_FFN = 64
HY_SLOW_RATE = -math.log(1e-2) / 1.5
HY_FAST_RATE = -math.log(1e-2) / 0.3
ML_HEADS = 4
ML_WIDTH = D_MIX - DA_HEADS * DA_V - GQ_HEADS * HEAD_DIM - HY_WIDTH
ML_HEAD_DIM = ML_WIDTH // ML_HEADS
ML_CHUNK = 64
M_EMPTY = -1e30
N_EXPERTS = 16
D_EXPERT = D_MODEL // 8
CAPACITY_FACTOR = 2
ADA_RANK = 512
N_MOD = 6
ALPHA = (2 * DEPTH) ** 0.25
BETA = (8 * DEPTH) ** -0.25
EPS = 1e-6

SEGMENTS = (
    ("da_q", DA_HEADS * 2 * HEAD_DIM),
    ("da_k", DA_HEADS * 2 * HEAD_DIM),
    ("da_v", DA_HEADS * DA_V),
    ("gq_q", GQ_HEADS * HEAD_DIM),
    ("gq_k", GQ_KV_HEADS * HEAD_DIM),
    ("gq_v", GQ_KV_HEADS * HEAD_DIM),
    ("hy_v", HY_WIDTH),
    ("hy_x1", HY_WIDTH),
    ("hy_x2", HY_WIDTH),
    ("ml_q", ML_WIDTH),
    ("ml_k", ML_WIDTH),
    ("ml_v", ML_WIDTH),
    ("ml_o", ML_WIDTH),
    ("ml_gates", 4 * ML_HEADS),
)
ALL_SEGS = tuple(name for name, _ in SEGMENTS)
CTX_STATE_SEGS = ("da_k", "da_v", "gq_k", "gq_v", "ml_k", "ml_v", "ml_gates")
N_IN = sum(width for _, width in SEGMENTS)

kernel_name = "hybrid_dit_prefix_trunk"


def layer_norm(x, g, b):
    xf = x.astype(jnp.float32)
    mu = jnp.mean(xf, axis=-1, keepdims=True)
    var = jnp.mean(jnp.square(xf - mu), axis=-1, keepdims=True)
    return ((xf - mu) * lax.rsqrt(var + 1e-5) * g + b).astype(x.dtype)


def rms_norm(x, g):
    xf = x.astype(jnp.float32)
    return (xf * lax.rsqrt(jnp.mean(jnp.square(xf), axis=-1, keepdims=True) + EPS) * g).astype(x.dtype)


def modulate(x, shift, scale):
    return x * (1.0 + scale) + shift


def adaln(cond, w_down, w_up, b):
    m = (jax.nn.silu(cond) @ w_down) @ w_up + b
    return m.reshape(cond.shape[:-1] + (N_MOD, D_MODEL))


def segment_table():
    table, off = {}, 0
    for name, width in SEGMENTS:
        table[name] = (off, width)
        off += width
    return table


def project(h, w_in, names):
    table = segment_table()
    if names == ALL_SEGS:
        w = w_in
    else:
        cols = np.concatenate([np.arange(table[n][0], table[n][0] + table[n][1]) for n in names])
        w = w_in[:, cols]
    y = jnp.einsum("btd,dn->btn", h, w)
    out, off = {}, 0
    for n in names:
        out[n] = y[..., off:off + table[n][1]]
        off += table[n][1]
    return out


def axial_rope(x):
    L, d = x.shape[1], x.shape[-1]
    rows = L // GRID_W
    row = jnp.repeat(jnp.arange(rows, dtype=jnp.float32), GRID_W)
    col = jnp.tile(jnp.arange(GRID_W, dtype=jnp.float32), rows)
    half = d // 2
    inv = ROPE_THETA ** (-jnp.arange(0, half, 2, dtype=jnp.float32) / half)
    ang = jnp.concatenate([row[:, None] * inv, col[:, None] * inv], axis=-1)
    cos, sin = jnp.cos(ang)[None, :, None, :], jnp.sin(ang)[None, :, None, :]
    xf = x.astype(jnp.float32)
    x1, x2 = xf[..., 0::2], xf[..., 1::2]
    out = jnp.stack([x1 * cos - x2 * sin, x1 * sin + x2 * cos], axis=-1)
    return out.reshape(x.shape).astype(x.dtype)


def sweep_blocks(fn, q):
    B, L = q.shape[0], q.shape[1]
    nb = L // Q_BLOCK
    qb = jnp.moveaxis(q.reshape((B, nb, Q_BLOCK) + q.shape[2:]), 1, 0)
    out = jnp.moveaxis(lax.map(fn, qb), 0, 1)
    return out.reshape((B, L) + out.shape[3:])


def diff_attend(q, k, v, lam):
    s = jnp.einsum("bqhmd,bkhmd->bhmqk", q, k) * (HEAD_DIM ** -0.5)
    p = jax.nn.softmax(s.astype(jnp.float32), axis=-1)
    a = p[:, :, 0] - lam * p[:, :, 1]
    return jnp.einsum("bhqk,bkhe->bqhe", a.astype(v.dtype), v)


def gqa_attend(q, k, v):
    s = jnp.einsum("bqgrd,bkgd->bgrqk", q, k) * (HEAD_DIM ** -0.5)
    p = jax.nn.softmax(s.astype(jnp.float32), axis=-1)
    return jnp.einsum("bgrqk,bkgd->bqgrd", p.astype(v.dtype), v)


def diff_attention(p, pc, lam_p, subln, layer, need_ctx):
    lam_init = 0.8 - 0.6 * math.exp(-0.3 * layer)
    lp = lam_p.astype(jnp.float32)
    lam = jnp.exp(jnp.sum(lp[0] * lp[1])) - jnp.exp(jnp.sum(lp[2] * lp[3])) + lam_init
    B, L = p["da_q"].shape[:2]
    Lc = pc["da_k"].shape[1]
    q = axial_rope(p["da_q"].reshape(B, L, 2 * DA_HEADS, HEAD_DIM)).reshape(B, L, DA_HEADS, 2, HEAD_DIM)
    k = axial_rope(p["da_k"].reshape(B, L, 2 * DA_HEADS, HEAD_DIM)).reshape(B, L, DA_HEADS, 2, HEAD_DIM)
    v = p["da_v"].reshape(B, L, DA_HEADS, DA_V)
    kc = pc["da_k"].reshape(B, Lc, DA_HEADS, 2, HEAD_DIM)
    vc = pc["da_v"].reshape(B, Lc, DA_HEADS, DA_V)
    k_all = jnp.concatenate([k, kc], axis=1)
    v_all = jnp.concatenate([v, vc], axis=1)
    out = sweep_blocks(lambda qb: diff_attend(qb, k_all, v_all, lam), q)

    def finish(o):
        return (rms_norm(o, subln) * (1.0 - lam_init)).reshape(o.shape[0], o.shape[1], DA_HEADS * DA_V)

    if not need_ctx:
        return finish(out), None
    qc = pc["da_q"].reshape(B, Lc, DA_HEADS, 2, HEAD_DIM)
    return finish(out), finish(diff_attend(qc, kc, vc, lam))


def axial_gqa(p, pc, qk_norm, need_ctx):
    B, L = p["gq_q"].shape[:2]
    Lc = pc["gq_k"].shape[1]
    rep = GQ_HEADS // GQ_KV_HEADS
    q = axial_rope(rms_norm(p["gq_q"].reshape(B, L, GQ_HEADS, HEAD_DIM), qk_norm[0]))
    q = q.reshape(B, L, GQ_KV_HEADS, rep, HEAD_DIM)
    k = axial_rope(rms_norm(p["gq_k"].reshape(B, L, GQ_KV_HEADS, HEAD_DIM), qk_norm[1]))
    v = p["gq_v"].reshape(B, L, GQ_KV_HEADS, HEAD_DIM)
    kc = rms_norm(pc["gq_k"].reshape(B, Lc, GQ_KV_HEADS, HEAD_DIM), qk_norm[1])
    vc = pc["gq_v"].reshape(B, Lc, GQ_KV_HEADS, HEAD_DIM)
    k_all = jnp.concatenate([k, kc], axis=1)
    v_all = jnp.concatenate([v, vc], axis=1)
    out = sweep_blocks(lambda qb: gqa_attend(qb, k_all, v_all), q).reshape(B, L, GQ_HEADS * HEAD_DIM)
    if not need_ctx:
        return out, None
    qc = rms_norm(pc["gq_q"].reshape(B, Lc, GQ_HEADS, HEAD_DIM), qk_norm[0]).reshape(B, Lc, GQ_KV_HEADS, rep, HEAD_DIM)
    return out, gqa_attend(qc, kc, vc).reshape(B, Lc, GQ_HEADS * HEAD_DIM)


def short_conv(u, w, b):
    y = lax.conv_general_dilated(
        u, w[:, None, :].astype(u.dtype), window_strides=(1,),
        padding=((HY_SHORT // 2, HY_SHORT // 2),),
        dimension_numbers=("NWC", "WIO", "NWC"), feature_group_count=u.shape[-1])
    return y + b


def hyena_filters(L, w1, b1, w2, b2, w3, sin_freq):
    bands = (HY_EMB - 1) // 2
    t = jnp.linspace(0.0, 1.0, L, dtype=jnp.float32)[:, None]
    w = (2.0 * math.pi / L) * jnp.arange(L, dtype=jnp.float32)[:, None]
    f = jnp.linspace(1e-4, bands - 1, bands, dtype=jnp.float32)[None, :]
    z = jnp.concatenate([t, jnp.cos(f * w), -jnp.sin(f * w)], axis=-1)
    a = jnp.sin(sin_freq[0] * (z @ w1 + b1))
    a = jnp.sin(sin_freq[1] * (a @ w2 + b2))
    h = (a @ w3).astype(jnp.float32).reshape(L, HY_ORDER, 2, HY_WIDTH)
    rates = jnp.linspace(HY_SLOW_RATE, HY_FAST_RATE, HY_WIDTH, dtype=jnp.float32)
    h = h * jnp.exp(-t[:, :, None, None] * rates)
    return h * lax.rsqrt(jnp.sum(jnp.square(h), axis=(0, 2), keepdims=True) + EPS)


def bidir_fftconv(u, hf, hb):
    L = u.shape[1]
    k = jnp.concatenate([hf, jnp.zeros_like(hf[:1]), hb[:0:-1]], axis=0)
    U = jnp.fft.rfft(u, n=2 * L, axis=1)
    K = jnp.fft.rfft(k, axis=0)
    return jnp.fft.irfft(U * K[None], n=2 * L, axis=1)[:, :L]


def hyena(p, conv_w, conv_b, f_w1, f_b1, f_w2, f_b2, f_w3, sin_freq, bias):
    u = short_conv(jnp.concatenate([p["hy_v"], p["hy_x1"], p["hy_x2"]], axis=-1), conv_w, conv_b)
    v, x1, x2 = jnp.split(u, 3, axis=-1)
    filt = hyena_filters(u.shape[1], f_w1, f_b1, f_w2, f_b2, f_w3, sin_freq)
    z = v.astype(jnp.float32)
    for o, gate in enumerate((x1, x2)):
        z = gate.astype(jnp.float32) * (bidir_fftconv(z, filt[:, o, 0], filt[:, o, 1]) + bias[o].astype(jnp.float32) * z)
    return z.astype(p["hy_v"].dtype)


def mlstm_scan(q, k, v, ig, lf, state):
    B, H, L, d = k.shape
    nc = L // ML_CHUNK
    mask = jnp.tril(jnp.ones((ML_CHUNK, ML_CHUNK), dtype=bool))

    def chunks(a):
        return None if a is None else jnp.moveaxis(a.reshape((B, H, nc, ML_CHUNK) + a.shape[3:]), 2, 0)

    def step(carry, xs):
        C, n, m = carry
        qc, kc, vc, ic, fc = xs
        b = jnp.cumsum(fc, axis=-1)
        m_end = jnp.maximum(b[..., -1] + m, jnp.max(b[..., -1:] - b + ic, axis=-1))
        w_end = jnp.exp(b[..., -1:] - b + ic - m_end[..., None])
        decay = jnp.exp(b[..., -1] + m - m_end)
        C_new = decay[..., None, None] * C + jnp.einsum("bhs,bhsv,bhsd->bhvd", w_end, vc, kc)
        n_new = decay[..., None] * n + jnp.einsum("bhs,bhsd->bhd", w_end, kc)
        if qc is None:
            return (C_new, n_new, m_end), None
        dlog = jnp.where(mask, b[..., :, None] - b[..., None, :] + ic[..., None, :], -jnp.inf)
        inter = b + m[..., None]
        m_row = jnp.maximum(inter, jnp.max(dlog, axis=-1))
        w_inter = jnp.exp(inter - m_row)
        s = jnp.einsum("bhjd,bhsd->bhjs", qc, kc) * jnp.exp(dlog - m_row[..., None])
        num = w_inter[..., None] * jnp.einsum("bhvd,bhjd->bhjv", C, qc) + jnp.einsum("bhjs,bhsv->bhjv", s, vc)
        den = w_inter * jnp.einsum("bhd,bhjd->bhj", n, qc) + jnp.sum(s, axis=-1)
        h = num / jnp.maximum(jnp.abs(den), jnp.exp(-m_row))[..., None]
        return (C_new, n_new, m_end), h

    state, hs = lax.scan(step, state, (chunks(q), chunks(k), chunks(v), chunks(ig), chunks(lf)))
    if q is None:
        return None, state
    return jnp.moveaxis(hs, 0, 2).reshape(B, H, L, d), state


def mlstm(p, pc, gate_b, norm_g, need_ctx):
    B, L = p["ml_k"].shape[:2]
    Lc = pc["ml_k"].shape[1]
    qscale = ML_HEAD_DIM ** -0.5

    def heads(a, Lx):
        return jnp.moveaxis(a.astype(jnp.float32).reshape(B, Lx, ML_HEADS, ML_HEAD_DIM), 1, 2)

    def gates(pp, Lx):
        g = (pp["ml_gates"] + gate_b).astype(jnp.float32).reshape(B, Lx, 2, 2, ML_HEADS)
        g = jnp.transpose(g, (2, 3, 0, 4, 1))
        return (g[0, 0], jax.nn.log_sigmoid(g[0, 1])), (g[1, 0], jax.nn.log_sigmoid(g[1, 1]))

    def flip(a):
        return None if a is None else jnp.flip(a, axis=2)

    def combine(hsum, o, Lx):
        hn = rms_norm(jnp.moveaxis(hsum, 1, 2), norm_g.reshape(ML_HEADS, ML_HEAD_DIM)).reshape(B, Lx, ML_WIDTH)
        return (jax.nn.sigmoid(o.astype(jnp.float32)) * hn).astype(o.dtype)

    empty = (jnp.zeros((B, ML_HEADS, ML_HEAD_DIM, ML_HEAD_DIM), jnp.float32),
             jnp.zeros((B, ML_HEADS, ML_HEAD_DIM), jnp.float32),
             jnp.full((B, ML_HEADS), M_EMPTY, jnp.float32))
    kc, vc = heads(pc["ml_k"], Lc), heads(pc["ml_v"], Lc)
    qc = heads(pc["ml_q"], Lc) * qscale if need_ctx else None
    (icf, lcf), (icb, lcb) = gates(pc, Lc)
    hcf, st_f = mlstm_scan(qc, kc, vc, icf, lcf, empty)
    hcb, st_b = mlstm_scan(flip(qc), flip(kc), flip(vc), flip(icb), flip(lcb), empty)
    q, k, v = heads(p["ml_q"], L) * qscale, heads(p["ml_k"], L), heads(p["ml_v"], L)
    (i_f, l_f), (i_b, l_b) = gates(p, L)
    hf, _ = mlstm_scan(q, k, v, i_f, l_f, st_f)
    hb, _ = mlstm_scan(flip(q), flip(k), flip(v), flip(i_b), flip(l_b), st_b)
    out = combine(hf + flip(hb), p["ml_o"], L)
    if not need_ctx:
        return out, None
    return out, combine(hcf + flip(hcb), pc["ml_o"], Lc)


def token_mixers(h, hc, layer, need_ctx, w_in, w_out, da_lambda, da_subln, gq_qk_norm, hy_conv_w, hy_conv_b,
                 hy_f_w1, hy_f_b1, hy_f_w2, hy_f_b2, hy_f_w3, hy_sin_freq, hy_bias, ml_gate_b, ml_norm):
    p = project(h, w_in, ALL_SEGS)
    pc = project(hc, w_in, ALL_SEGS if need_ctx else CTX_STATE_SEGS)
    da, da_c = diff_attention(p, pc, da_lambda, da_subln, layer, need_ctx)
    gq, gq_c = axial_gqa(p, pc, gq_qk_norm, need_ctx)
    hy = hyena(p, hy_conv_w, hy_conv_b, hy_f_w1, hy_f_b1, hy_f_w2, hy_f_b2, hy_f_w3, hy_sin_freq, hy_bias)
    ml, ml_c = mlstm(p, pc, ml_gate_b, ml_norm, need_ctx)
    out = jnp.concatenate([da, gq, hy, ml], axis=-1) @ w_out
    if not need_ctx:
        return out, None
    hy_c = hyena(pc, hy_conv_w, hy_conv_b, hy_f_w1, hy_f_b1, hy_f_w2, hy_f_b2, hy_f_w3, hy_sin_freq, hy_bias)
    out_c = jnp.concatenate([da_c, gq_c, hy_c, ml_c], axis=-1) @ w_out
    return out, out_c


def expert_choice_ffn(h, router, w1, w3, w2):
    B, T, _ = h.shape
    cap = CAPACITY_FACTOR * T // N_EXPERTS
    aff = jax.nn.softmax(jnp.einsum("btd,de->bte", h, router).astype(jnp.float32), axis=-1)
    gate, idx = lax.top_k(jnp.swapaxes(aff, 1, 2), cap)
    bidx = jnp.arange(B)[:, None, None]
    xs = h[bidx, idx]
    hid = jax.nn.silu(jnp.einsum("becd,edf->becf", xs, w1)) * jnp.einsum("becd,edf->becf", xs, w3)
    y = jnp.einsum("becf,efd->becd", hid, w2) * gate[..., None].astype(h.dtype)
    return jnp.zeros_like(h).at[bidx, idx].add(y)


def setup_inputs(seed: int = 0) -> dict:
    key = jax.random.key(seed)
    ks = iter(jax.random.split(key, 40))

    def nrm(shape, scale):
        return scale * jax.random.normal(next(ks), shape, jnp.float32)

    D, E, F = D_MODEL, N_EXPERTS, D_EXPERT
    ig_b = nrm((DEPTH, 2, 1, ML_HEADS), 0.1)
    fg_b = 3.0 + 3.0 * jax.random.uniform(next(ks), (DEPTH, 2, 1, ML_HEADS), jnp.float32)
    ml_gate_b = jnp.concatenate([ig_b, fg_b], axis=2).reshape(DEPTH, 4 * ML_HEADS)
    return {
        "x": nrm((BATCH, SEQ, D), 1.0),
        "c": nrm((BATCH, D), 1.0),
        "ctx": nrm((BATCH, CTX_LEN, D), 1.0),
        "c_ctx": nrm((D,), 1.0),
        "ada_down": nrm((DEPTH, D, ADA_RANK), D ** -0.5),
        "ada_up": nrm((DEPTH, ADA_RANK, N_MOD * D), 0.5 * ADA_RANK ** -0.5),
        "ada_b": nrm((DEPTH, N_MOD * D), 0.02),
        "w_in": nrm((DEPTH, D, N_IN), D ** -0.5),
        "w_out": nrm((DEPTH, D_MIX, D), BETA * D_MIX ** -0.5),
        "da_lambda": nrm((DEPTH, 4, HEAD_DIM), 0.1),
        "da_subln": 1.0 + nrm((DEPTH, DA_V), 0.02),
        "gq_qk_norm": 1.0 + nrm((DEPTH, 2, HEAD_DIM), 0.02),
        "hy_conv_w": nrm((DEPTH, HY_SHORT, 3 * HY_WIDTH), HY_SHORT ** -0.5),
        "hy_conv_b": nrm((DEPTH, 3 * HY_WIDTH), 0.02),
        "hy_f_w1": nrm((DEPTH, HY_EMB, HY_FFN), HY_EMB ** -0.5),
        "hy_f_b1": nrm((DEPTH, HY_FFN), 0.02),
        "hy_f_w2": nrm((DEPTH, HY_FFN, HY_FFN), HY_FFN ** -0.5),
        "hy_f_b2": nrm((DEPTH, HY_FFN), 0.02),
        "hy_f_w3": nrm((DEPTH, HY_FFN, HY_ORDER * 2 * HY_WIDTH), HY_FFN ** -0.5),
        "hy_sin_freq": 1.0 + nrm((DEPTH, 2, HY_FFN), 0.02),
        "hy_bias": nrm((DEPTH, HY_ORDER, HY_WIDTH), 0.1),
        "ml_gate_b": ml_gate_b,
        "ml_norm": 1.0 + nrm((DEPTH, ML_WIDTH), 0.02),
        "ln_g": 1.0 + nrm((DEPTH, 2, D), 0.02),
        "ln_b": nrm((DEPTH, 2, D), 0.02),
        "router": nrm((DEPTH, D, E), D ** -0.5),
        "ex_w1": nrm((DEPTH, E, D, F), D ** -0.5),
        "ex_w3": nrm((DEPTH, E, D, F), D ** -0.5),
        "ex_w2": nrm((DEPTH, E, F, D), BETA * F ** -0.5),
    }


def reference(x, c, ctx, c_ctx, ada_down, ada_up, ada_b, w_in, w_out, da_lambda, da_subln, gq_qk_norm,
              hy_conv_w, hy_conv_b, hy_f_w1, hy_f_b1, hy_f_w2, hy_f_b2, hy_f_w3, hy_sin_freq, hy_bias,
              ml_gate_b, ml_norm, ln_g, ln_b, router, ex_w1, ex_w3, ex_w2):
    for layer in range(DEPTH):
        last = layer == DEPTH - 1
        mod = adaln(c, ada_down[layer], ada_up[layer], ada_b[layer])[:, :, None, :]
        mod_c = adaln(c_ctx, ada_down[layer], ada_up[layer], ada_b[layer])
        h = modulate(x, mod[:, 0], mod[:, 1])
        hc = modulate(ctx, mod_c[0], mod_c[1])
        mix, mix_c = token_mixers(
            h, hc, layer, not last, w_in[layer], w_out[layer], da_lambda[layer], da_subln[layer],
            gq_qk_norm[layer], hy_conv_w[layer], hy_conv_b[layer], hy_f_w1[layer], hy_f_b1[layer],
            hy_f_w2[layer], hy_f_b2[layer], hy_f_w3[layer], hy_sin_freq[layer], hy_bias[layer],
            ml_gate_b[layer], ml_norm[layer])
        x = layer_norm(ALPHA * x + mod[:, 2] * mix, ln_g[layer, 0], ln_b[layer, 0])
        ffn = expert_choice_ffn(modulate(x, mod[:, 3], mod[:, 4]), router[layer], ex_w1[layer], ex_w3[layer], ex_w2[layer])
        x = layer_norm(ALPHA * x + mod[:, 5] * ffn, ln_g[layer, 1], ln_b[layer, 1])
        if not last:
            ctx = layer_norm(ALPHA * ctx + mod_c[2] * mix_c, ln_g[layer, 0], ln_b[layer, 0])
            ffn_c = expert_choice_ffn(modulate(ctx, mod_c[3], mod_c[4]), router[layer], ex_w1[layer], ex_w3[layer], ex_w2[layer])
            ctx = layer_norm(ALPHA * ctx + mod_c[5] * ffn_c, ln_g[layer, 1], ln_b[layer, 1])
    return x
```

```python
import functools
import math

import jax
import jax.numpy as jnp
import numpy as np
from jax import lax
from jax.experimental import pallas as pl
from jax.experimental.pallas import tpu as pltpu

HEAD_DIM = 128
GRID_W = 64
ROPE_THETA = 10000.0
HY_ORDER = 2
HY_EMB = 33
HY_SLOW_RATE = -math.log(1e-2) / 1.5
HY_FAST_RATE = -math.log(1e-2) / 0.3
ML_HEADS = 4
M_EMPTY = -1e30
CAPACITY_FACTOR = 2
N_MOD = 6
EPS = 1e-6
LN_EPS = 1e-5

V7X_LANES = 128
V7X_SUBLANES = 8
V7X_VMEM_BYTES = 64 * 1024 * 1024
V7X_VMEM_REQUEST_CAP = 56 * 1024 * 1024
MIN_VMEM_REQUEST = 32 * 1024 * 1024

BF16 = jnp.bfloat16
F32 = jnp.float32


def _pick_tile(n, pref, mult=V7X_LANES):
    if n <= pref:
        return n
    t = (pref // mult) * mult
    while t >= mult:
        if n % t == 0:
            return t
        t -= mult
    return n


def _vmem_limit(block_bytes):
    return int(min(max(block_bytes * 5 // 4, MIN_VMEM_REQUEST), V7X_VMEM_REQUEST_CAP))


def _mm_body(a_ref, b_ref, o_ref, *acc, nk, kaxis):
    prod = jnp.dot(a_ref[...].astype(BF16), b_ref[...].astype(BF16), preferred_element_type=F32)
    if nk == 1:
        o_ref[...] = prod.astype(o_ref.dtype)
        return
    (acc_ref,) = acc
    k = pl.program_id(kaxis)

    @pl.when(k == 0)
    def _():
        acc_ref[...] = prod

    @pl.when(k > 0)
    def _():
        acc_ref[...] += prod

    @pl.when(k == nk - 1)
    def _():
        o_ref[...] = acc_ref[...].astype(o_ref.dtype)


def _matmul(a, b, out_dtype=F32, tm=1024, tn=512, tk=4096, name="matmul"):
    M, K = a.shape
    batched = b.ndim == 3
    N = b.shape[-1]
    assert b.shape[-2] == K
    tm = _pick_tile(M, tm, V7X_SUBLANES)
    tn = _pick_tile(N, tn)
    tk = _pick_tile(K, tk)
    nk = K // tk
    abytes = tm * tk * a.dtype.itemsize
    bbytes = tk * tn * b.dtype.itemsize
    obytes = tm * tn * jnp.dtype(out_dtype).itemsize
    vmem = _vmem_limit(2 * (abytes + bbytes + obytes) + (tm * tn * 4 if nk > 1 else 0)
                       + tm * tn * 4 + (tm * tk + tk * tn) * 2)
    scratch = [pltpu.VMEM((tm, tn), F32)] if nk > 1 else []
    if batched:
        G = b.shape[0]
        grid = (M // tm, G, N // tn, nk)
        in_specs = [pl.BlockSpec((tm, tk), lambda i, g, j, k: (i, k)),
                    pl.BlockSpec((None, tk, tn), lambda i, g, j, k: (g, k, j))]
        out_specs = pl.BlockSpec((None, tm, tn), lambda i, g, j, k: (g, i, j))
        out_shape = jax.ShapeDtypeStruct((G, M, N), out_dtype)
        sem = ("parallel", "parallel", "parallel", "arbitrary")
        kaxis = 3
    else:
        grid = (M // tm, N // tn, nk)
        in_specs = [pl.BlockSpec((tm, tk), lambda i, j, k: (i, k)),
                    pl.BlockSpec((tk, tn), lambda i, j, k: (k, j))]
        out_specs = pl.BlockSpec((tm, tn), lambda i, j, k: (i, j))
        out_shape = jax.ShapeDtypeStruct((M, N), out_dtype)
        sem = ("parallel", "parallel", "arbitrary")
        kaxis = 2
    return pl.pallas_call(
        functools.partial(_mm_body, nk=nk, kaxis=kaxis),
        out_shape=out_shape, grid=grid, in_specs=in_specs, out_specs=out_specs,
        scratch_shapes=scratch, name=name,
        compiler_params=pltpu.CompilerParams(dimension_semantics=sem, vmem_limit_bytes=vmem),
    )(a, b)


def _mod_index(b, i, nlat, nb):
    return jnp.where(i < nlat, b, nb)


def _modulate_body(x_ref, m_ref, h_ref, *, shift, scale):
    m = m_ref[...]
    h_ref[...] = (x_ref[...] * (1.0 + m[scale:scale + 1]) + m[shift:shift + 1]).astype(h_ref.dtype)


def _modulate(x, mods, L, shift, scale, out_dtype=BF16, tr=256):
    B, Lt, D = x.shape
    tr = _pick_tile(math.gcd(L, Lt - L), tr, V7X_SUBLANES)
    nlat = L // tr
    blk = tr * D * (4 + jnp.dtype(out_dtype).itemsize)
    return pl.pallas_call(
        functools.partial(_modulate_body, shift=shift, scale=scale),
        out_shape=jax.ShapeDtypeStruct((B, Lt, D), out_dtype),
        grid=(B, Lt // tr),
        in_specs=[pl.BlockSpec((None, tr, D), lambda b, i: (b, i, 0)),
                  pl.BlockSpec((None, N_MOD, D), lambda b, i: (_mod_index(b, i, nlat, B), 0, 0))],
        out_specs=pl.BlockSpec((None, tr, D), lambda b, i: (b, i, 0)),
        name="modulate",
        compiler_params=pltpu.CompilerParams(dimension_semantics=("parallel", "parallel"),
                                             vmem_limit_bytes=_vmem_limit(2 * blk)),
    )(x, mods)


def _resid_ln_body(x_ref, y_ref, m_ref, m2_ref, g_ref, b_ref, xo_ref, *h_ref, alpha, gate, shift, scale):
    m = m_ref[...]
    v = alpha * x_ref[...] + m[gate:gate + 1] * y_ref[...]
    mu = jnp.mean(v, axis=-1, keepdims=True)
    vc = v - mu
    var = jnp.mean(vc * vc, axis=-1, keepdims=True)
    out = vc * lax.rsqrt(var + LN_EPS) * g_ref[...] + b_ref[...]
    xo_ref[...] = out
    if h_ref:
        m2 = m2_ref[...]
        h_ref[0][...] = (out * (1.0 + m2[scale:scale + 1]) + m2[shift:shift + 1]).astype(h_ref[0].dtype)


def _resid_ln(x, y, mods, mods2, g, b, L, alpha, gate, shift=None, scale=None, h_dtype=F32, tr=128):
    B, Lt, D = x.shape
    tr = _pick_tile(math.gcd(L, Lt - L), tr, V7X_SUBLANES)
    nlat = L // tr
    emit_h = shift is not None
    row = pl.BlockSpec((None, tr, D), lambda b_, i: (b_, i, 0))
    modspec = pl.BlockSpec((None, N_MOD, D), lambda b_, i: (_mod_index(b_, i, nlat, B), 0, 0))
    vec = pl.BlockSpec((1, D), lambda b_, i: (0, 0))
    out_shape = [jax.ShapeDtypeStruct((B, Lt, D), F32)]
    out_specs = [row]
    if emit_h:
        out_shape.append(jax.ShapeDtypeStruct((B, Lt, D), h_dtype))
        out_specs.append(row)
    blk = tr * D * 4 * 4
    res = pl.pallas_call(
        functools.partial(_resid_ln_body, alpha=alpha, gate=gate, shift=shift, scale=scale),
        out_shape=out_shape, grid=(B, Lt // tr),
        in_specs=[row, row, modspec, modspec, vec, vec], out_specs=out_specs,
        name="resid_ln",
        compiler_params=pltpu.CompilerParams(dimension_semantics=("parallel", "parallel"),
                                             vmem_limit_bytes=_vmem_limit(2 * blk)),
    )(x, y, mods, mods2, g.reshape(1, D), b.reshape(1, D))
    return (res[0], res[1]) if emit_h else (res[0], None)


def _qk_prep_body(p_ref, cos_ref, sin_ref, gain_ref, o_ref, *, plan):
    cos = cos_ref[...]
    sin = sin_ref[...]
    tr = cos.shape[0]
    lane = lax.broadcasted_iota(jnp.int32, (tr, HEAD_DIM), 1)
    partner = lane ^ 1
    from_prev = pltpu.roll(lane, 1, axis=1) == partner
    for blk, (norm_row, rope, post) in enumerate(plan):
        sl = slice(blk * HEAD_DIM, (blk + 1) * HEAD_DIM)
        v = p_ref[:, sl]
        if norm_row is not None:
            ms = jnp.mean(v * v, axis=-1, keepdims=True)
            v = v * lax.rsqrt(ms + EPS) * gain_ref[norm_row:norm_row + 1, :]
        if rope:
            swapped = jnp.where(from_prev, pltpu.roll(v, 1, axis=1), pltpu.roll(v, HEAD_DIM - 1, axis=1))
            v = v * cos + swapped * sin
        if post != 1.0:
            v = v * post
        o_ref[:, sl] = v.astype(o_ref.dtype)


def _qk_prep(p, cos, sin, gains, plan, tr=256):
    B, Lt, _ = p.shape
    W = len(plan) * HEAD_DIM
    tr = _pick_tile(Lt, tr, 16)
    blk = tr * W * (4 + 2) + 2 * tr * HEAD_DIM * 4
    return pl.pallas_call(
        functools.partial(_qk_prep_body, plan=tuple(plan)),
        out_shape=jax.ShapeDtypeStruct((B, Lt, W), BF16),
        grid=(B, Lt // tr),
        in_specs=[pl.BlockSpec((None, tr, W), lambda b, i: (b, i, 0)),
                  pl.BlockSpec((tr, HEAD_DIM), lambda b, i: (i, 0)),
                  pl.BlockSpec((tr, HEAD_DIM), lambda b, i: (i, 0)),
                  pl.BlockSpec(gains.shape, lambda b, i: (0, 0))],
        out_specs=pl.BlockSpec((None, tr, W), lambda b, i: (b, i, 0)),
        name="qk_prep",
        compiler_params=pltpu.CompilerParams(dimension_semantics=("parallel", "parallel"),
                                             vmem_limit_bytes=_vmem_limit(2 * blk)),
    )(p, cos, sin, gains)


def _attn_body(q_ref, k_ref, v_ref, o_ref):
    s = lax.dot_general(q_ref[...], k_ref[...], (((1,), (1,)), ((), ())), preferred_element_type=F32)
    m = jnp.max(s, axis=-1, keepdims=True)
    e = jnp.exp(s - m)
    l = jnp.sum(e, axis=-1, keepdims=True)
    o = jnp.dot(e.astype(BF16), v_ref[...], preferred_element_type=F32)
    o_ref[...] = o / l


def _attention(a, nheads, q_col, k_col, k_div, v_col, v_div, dv, q_rows, k_rows, tq=512):
    B = a.shape[0]
    q0, Lq = q_rows
    k0, Lk = k_rows
    tq = _pick_tile(Lq, tq, 16)
    assert q0 % tq == 0 and k0 % Lk == 0
    qb0, kb0 = q0 // tq, k0 // Lk
    blk = tq * HEAD_DIM * 2 + Lk * (HEAD_DIM + dv) * 2 + tq * dv * 4
    inter = tq * Lk * (4 + 4 + 2)
    return pl.pallas_call(
        _attn_body,
        out_shape=jax.ShapeDtypeStruct((B, Lq, nheads * dv), F32),
        grid=(B, nheads, Lq // tq),
        in_specs=[pl.BlockSpec((None, tq, HEAD_DIM), lambda b, h, i: (b, qb0 + i, q_col + h)),
                  pl.BlockSpec((None, Lk, HEAD_DIM), lambda b, h, i: (b, kb0, k_col + h // k_div)),
                  pl.BlockSpec((None, Lk, dv), lambda b, h, i: (b, kb0, v_col + h // v_div))],
        out_specs=pl.BlockSpec((None, tq, dv), lambda b, h, i: (b, i, h)),
        name="attention",
        compiler_params=pltpu.CompilerParams(dimension_semantics=("parallel", "parallel", "parallel"),
                                             vmem_limit_bytes=_vmem_limit(2 * blk + inter)),
    )(a, a, a)


def _mlstm_dir(q, k, v, ig, fpre, c_ref, n_ref, m_ref, reverse):
    S = q.shape[0]
    lf = jax.nn.log_sigmoid(fpre)
    r = lax.broadcasted_iota(jnp.int32, (S, S), 0)
    c = lax.broadcasted_iota(jnp.int32, (S, S), 1)
    eye = r == c
    seen = (c >= r) if reverse else (c <= r)
    seen_t = (r >= c) if reverse else (r <= c)
    zero = jnp.zeros((S, S), F32)
    lf_row = jnp.sum(jnp.where(eye, lf, zero), axis=0, keepdims=True)
    ig_row = jnp.sum(jnp.where(eye, ig, zero), axis=0, keepdims=True)
    b_col = jnp.sum(jnp.where(seen, lf_row, zero), axis=1, keepdims=True)
    b_row = jnp.sum(jnp.where(seen_t, lf, zero), axis=0, keepdims=True)
    total = jnp.sum(lf, axis=0, keepdims=True)
    m_prev = m_ref[...]
    g_row = total - b_row + ig_row
    g_col = total - b_col + ig
    m_end = jnp.maximum(total + m_prev, jnp.max(g_row, axis=1, keepdims=True))
    w_end = jnp.exp(g_col - m_end)
    decay = jnp.exp(total + m_prev - m_end)
    dlog = jnp.where(seen, b_col - b_row + ig_row, -jnp.inf)
    inter = b_col + m_prev
    m_row = jnp.maximum(inter, jnp.max(dlog, axis=1, keepdims=True))
    w_inter = jnp.exp(inter - m_row)
    qb, kb, vb = q.astype(BF16), k.astype(BF16), v.astype(BF16)
    c_old = c_ref[...]
    n_old = n_ref[...]
    qk = lax.dot_general(qb, kb, (((1,), (1,)), ((), ())), preferred_element_type=F32)
    s = qk * jnp.exp(dlog - m_row)
    cq = lax.dot_general(qb, c_old.astype(BF16), (((1,), (1,)), ((), ())), preferred_element_type=F32)
    num = w_inter * cq + jnp.dot(s.astype(BF16), vb, preferred_element_type=F32)
    den = w_inter * jnp.sum(q * n_old, axis=1, keepdims=True) + jnp.sum(s, axis=1, keepdims=True)
    h = num / jnp.maximum(jnp.abs(den), jnp.exp(-m_row))
    vw_t = (v * w_end).T.astype(BF16)
    c_ref[...] = decay * c_old + jnp.dot(vw_t, kb, preferred_element_type=F32)
    n_ref[...] = decay * n_old + jnp.sum(k * w_end, axis=0, keepdims=True)
    m_ref[...] = m_end
    return h


def _mlstm_body(qf_ref, kf_ref, vf_ref, gf_ref, qb_ref, kb_ref, vb_ref, gb_ref, hf_ref, hb_ref,
                cf, nf, mf, cb, nb, mb, *, qscale):
    @pl.when(pl.program_id(2) == 0)
    def _():
        for c_ref, n_ref, m_ref in ((cf, nf, mf), (cb, nb, mb)):
            c_ref[...] = jnp.zeros_like(c_ref)
            n_ref[...] = jnp.zeros_like(n_ref)
            m_ref[...] = jnp.full_like(m_ref, M_EMPTY)

    gf = gf_ref[...]
    gb = gb_ref[...]
    hf_ref[...] = _mlstm_dir(qf_ref[...] * qscale, kf_ref[...], vf_ref[...], gf[:, 0:1], gf[:, 1:2],
                             cf, nf, mf, False)
    hb_ref[...] = _mlstm_dir(qb_ref[...] * qscale, kb_ref[...], vb_ref[...], gb[:, 2:3], gb[:, 3:4],
                             cb, nb, mb, True)


def _mlstm(p, gates, L, q_col, k_col, v_col, d, S):
    B, Lt, _ = p.shape
    H = gates.shape[1]
    nch = Lt // S
    nlat = L // S

    def fwd(t):
        return jnp.where(t < nch - nlat, nlat + t, t - (nch - nlat))

    def bwd(t):
        return nch - 1 - t

    def seg(col, order):
        return pl.BlockSpec((None, S, d), lambda b, h, t: (b, order(t), col + h))

    def gate(order):
        return pl.BlockSpec((None, None, S, 4), lambda b, h, t: (b, h, order(t), 0))

    out_f = pl.BlockSpec((None, S, d), lambda b, h, t: (b, fwd(t), h))
    out_b = pl.BlockSpec((None, S, d), lambda b, h, t: (b, bwd(t), h))
    state = [pltpu.VMEM((d, d), F32), pltpu.VMEM((1, d), F32), pltpu.VMEM((1, 1), F32)]
    blk = 8 * S * d * 4 + 2 * S * V7X_LANES * 4
    return pl.pallas_call(
        functools.partial(_mlstm_body, qscale=d ** -0.5),
        out_shape=[jax.ShapeDtypeStruct((B, Lt, H * d), F32)] * 2,
        grid=(B, H, nch),
        in_specs=[seg(q_col, fwd), seg(k_col, fwd), seg(v_col, fwd), gate(fwd),
                  seg(q_col, bwd), seg(k_col, bwd), seg(v_col, bwd), gate(bwd)],
        out_specs=[out_f, out_b],
        scratch_shapes=state + state,
        name="mlstm",
        compiler_params=pltpu.CompilerParams(dimension_semantics=("parallel", "parallel", "arbitrary"),
                                             vmem_limit_bytes=_vmem_limit(2 * blk + 16 * S * S * 4 + 4 * d * d * 4)),
    )(p, p, p, gates, p, p, p, gates)


def _ffn_row_copy(h_hbm, xs_ref, sem, b, src_row, dst_row):
    return pltpu.make_async_copy(h_hbm.at[b, pl.ds(src_row, 1), :], xs_ref.at[pl.ds(dst_row, 1), :], sem)


def _expert_ffn_body(idx_ref, gate_ref, h_hbm, w1_ref, w3_ref, w2_ref, y_ref, xs_ref, hid_ref, sem, *, rows):
    b = pl.program_id(1)

    @pl.when(pl.program_id(2) == 0)
    def _():
        def issue(r, carry):
            _ffn_row_copy(h_hbm, xs_ref, sem, b, idx_ref[0, 0, r], r).start()
            return carry

        lax.fori_loop(0, rows, issue, 0)

        def drain(r, carry):
            _ffn_row_copy(h_hbm, xs_ref, sem, b, 0, r).wait()
            return carry

        lax.fori_loop(0, rows, drain, 0)
        xs = xs_ref[...].astype(BF16)
        a = jnp.dot(xs, w1_ref[...], preferred_element_type=F32)
        g = jnp.dot(xs, w3_ref[...], preferred_element_type=F32)
        hid_ref[...] = (jax.nn.silu(a) * g).astype(BF16)

    y = jnp.dot(hid_ref[...], w2_ref[...], preferred_element_type=F32)
    y_ref[...] = y * gate_ref[...]


def _expert_ffn(hm, idx, gate, w1, w3, w2, tn=1024):
    B, _, D = hm.shape
    E, _, F = w1.shape
    R = idx.shape[-1]
    tn = _pick_tile(D, tn)
    blk = 2 * D * F * 2 + F * tn * 2 + R * tn * 4 + R * V7X_LANES * 4
    return pl.pallas_call(
        functools.partial(_expert_ffn_body, rows=R),
        out_shape=jax.ShapeDtypeStruct((B * E, R, D), F32),
        grid=(E, B, D // tn),
        in_specs=[pl.BlockSpec((1, 1, R), lambda e, b, j: (b * E + e, 0, 0), memory_space=pltpu.SMEM),
                  pl.BlockSpec((None, R, 1), lambda e, b, j: (b * E + e, 0, 0)),
                  pl.BlockSpec(memory_space=pl.ANY),
                  pl.BlockSpec((None, D, F), lambda e, b, j: (e, 0, 0)),
                  pl.BlockSpec((None, D, F), lambda e, b, j: (e, 0, 0)),
                  pl.BlockSpec((None, F, tn), lambda e, b, j: (e, 0, j))],
        out_specs=pl.BlockSpec((None, R, tn), lambda e, b, j: (b * E + e, 0, j)),
        scratch_shapes=[pltpu.VMEM((R, D), F32), pltpu.VMEM((R, F), BF16), pltpu.SemaphoreType.DMA(())],
        name="expert_ffn",
        compiler_params=pltpu.CompilerParams(dimension_semantics=("arbitrary", "arbitrary", "arbitrary"),
                                             vmem_limit_bytes=_vmem_limit(2 * blk + R * D * 6 + R * F * 10)),
    )(idx, gate, hm, w1, w3, w2)


def _rope_tables(L, Lc):
    rows = L // GRID_W
    row = jnp.repeat(jnp.arange(rows, dtype=F32), GRID_W)
    col = jnp.tile(jnp.arange(GRID_W, dtype=F32), rows)
    half = HEAD_DIM // 2
    inv = ROPE_THETA ** (-jnp.arange(0, half, 2, dtype=F32) / half)
    ang = jnp.concatenate([row[:, None] * inv, col[:, None] * inv], axis=-1)
    cos, sin = jnp.cos(ang), jnp.sin(ang)
    cos2 = jnp.repeat(cos, 2, axis=-1)
    sin2 = jnp.stack([-sin, sin], axis=-1).reshape(L, HEAD_DIM)
    cos2 = jnp.concatenate([cos2, jnp.ones((Lc, HEAD_DIM), F32)], axis=0)
    sin2 = jnp.concatenate([sin2, jnp.zeros((Lc, HEAD_DIM), F32)], axis=0)
    return cos2, sin2


def _dft_matrices(L):
    N = 2 * L
    k = jnp.arange(L, dtype=jnp.int32)
    n = jnp.arange(L, dtype=jnp.int32)
    ang = (2.0 * math.pi / N) * ((k[:, None] * n[None, :]) % N).astype(F32)
    cos, sin = jnp.cos(ang), jnp.sin(ang)
    nyq = jnp.where(n % 2 == 0, 1.0, -1.0).astype(F32)[None, :]
    first = (k == 0)[:, None]
    fwd = jnp.concatenate([cos, jnp.where(first, nyq, -sin)], axis=0)
    inv_re = jnp.where(first, 1.0, 2.0) * cos / N
    inv_im = jnp.where(first, nyq / N, -2.0 * sin / N)
    inv = jnp.concatenate([inv_re, inv_im], axis=0).T
    return fwd.astype(BF16), inv.astype(BF16)


def _pad_to(a, axis, size):
    pad = [(0, 0)] * a.ndim
    pad[axis] = (0, size - a.shape[axis])
    return jnp.pad(a, pad)


def _hyena_filters(L, w1, b1, w2, b2, w3, sin_freq, width):
    bands = (HY_EMB - 1) // 2
    t = jnp.linspace(0.0, 1.0, L, dtype=F32)[:, None]
    w = (2.0 * math.pi / L) * jnp.arange(L, dtype=F32)[:, None]
    f = jnp.linspace(1e-4, bands - 1, bands, dtype=F32)[None, :]
    z = jnp.concatenate([t, jnp.cos(f * w), -jnp.sin(f * w)], axis=-1)
    ffn = w1.shape[1]
    a = _matmul(_pad_to(z, 1, V7X_LANES), _pad_to(_pad_to(w1, 0, V7X_LANES), 1, V7X_LANES), name="hy_f1")
    a = jnp.sin(sin_freq[0] * (a[:, :ffn] + b1))
    a = _matmul(_pad_to(a, 1, V7X_LANES), _pad_to(_pad_to(w2, 0, V7X_LANES), 1, V7X_LANES), name="hy_f2")
    a = jnp.sin(sin_freq[1] * (a[:, :ffn] + b2))
    h = _matmul(_pad_to(a, 1, V7X_LANES), _pad_to(w3, 0, V7X_LANES), name="hy_f3")
    h = h.reshape(L, HY_ORDER, 2, width)
    rates = jnp.linspace(HY_SLOW_RATE, HY_FAST_RATE, width, dtype=F32)
    h = h * jnp.exp(-t[:, :, None, None] * rates)
    return h * lax.rsqrt(jnp.sum(jnp.square(h), axis=(0, 2), keepdims=True) + EPS)


def _spec_mul(U, K, L):
    first = (jnp.arange(L) == 0)[:, None]
    Ur, Ui = U[..., :L, :], U[..., L:, :]
    Kr, Ki = K[..., :L, :], K[..., L:, :]
    ii = Ui * Ki
    Yr = Ur * Kr - jnp.where(first, 0.0, ii)
    Yi = jnp.where(first, ii, Ur * Ki + Ui * Kr)
    return jnp.concatenate([Yr, Yi], axis=-2)


def _hyena(u3, conv_w, conv_b, filt, bias, dft):
    B, L, C3 = u3.shape
    C = C3 // 3
    fwd, inv = dft
    up = jnp.pad(u3, ((0, 0), (1, 1), (0, 0)))
    u = conv_w[0] * up[:, :-2] + conv_w[1] * up[:, 1:-1] + conv_w[2] * up[:, 2:] + conv_b
    v, x1, x2 = u[..., :C], u[..., C:2 * C], u[..., 2 * C:]
    first = (jnp.arange(L) == 0)[:, None]
    cols = []
    for o in range(HY_ORDER):
        cols += [filt[:, o, 0], jnp.where(first, 0.0, filt[:, o, 1])]
    H = _matmul(fwd, jnp.stack(cols).astype(BF16), name="hy_dft_filter")
    z = v
    for o, gate in enumerate((x1, x2)):
        Hf, Hb = H[2 * o], H[2 * o + 1]
        Kr = Hf[:L] + Hb[:L]
        Ki = jnp.where(first, Hf[L:] + Hb[L:], Hf[L:] - Hb[L:])
        K = jnp.concatenate([Kr, Ki], axis=0)
        U = _matmul(fwd, z.astype(BF16), name="hy_dft_fwd")
        Y = _spec_mul(U, K, L).astype(BF16)
        conv = _matmul(inv, Y, name="hy_dft_inv")
        z = gate * (conv + bias[o] * z)
    return z


def _rms(x, g):
    return x * lax.rsqrt(jnp.mean(jnp.square(x), axis=-1, keepdims=True) + EPS) * g


def kernel(x, c, ctx, c_ctx, ada_down, ada_up, ada_b, w_in, w_out, da_lambda, da_subln, gq_qk_norm,
           hy_conv_w, hy_conv_b, hy_f_w1, hy_f_b1, hy_f_w2, hy_f_b2, hy_f_w3, hy_sin_freq, hy_bias,
           ml_gate_b, ml_norm, ln_g, ln_b, router, ex_w1, ex_w3, ex_w2):
    B, L, D = x.shape
    Lc = ctx.shape[1]
    Lt = L + Lc
    depth = w_in.shape[0]
    E = router.shape[-1]
    alpha = (2 * depth) ** 0.25
    da_heads = D // (8 * HEAD_DIM)
    da_v = 2 * HEAD_DIM
    gq_heads = D // (4 * HEAD_DIM)
    gq_kv = gq_heads // 4
    gq_rep = gq_heads // gq_kv
    hy_w = D // 4
    ml_w = D - da_heads * da_v - gq_heads * HEAD_DIM - hy_w
    ml_d = ml_w // ML_HEADS
    widths = [da_heads * 2 * HEAD_DIM, da_heads * 2 * HEAD_DIM, da_heads * da_v, gq_heads * HEAD_DIM,
              gq_kv * HEAD_DIM, gq_kv * HEAD_DIM, hy_w, hy_w, hy_w, ml_w, ml_w, ml_w, ml_w, 4 * ML_HEADS]
    offs = np.concatenate([[0], np.cumsum(widths)]).astype(int)
    o_daq, o_dak, o_dav, o_gqq, o_gqk, o_gqv, o_hy, _, _, o_mlq, o_mlk, o_mlv, o_mlo, o_mlg, n_in = offs
    n_pad = -(-n_in // 512) * 512
    att_w = o_hy
    S = min(256, Lc)
    assert L % S == 0 and Lc % S == 0 and L % Lc == 0

    qs = HEAD_DIM ** -0.5
    n_da = 2 * da_heads
    plan = ([(None, True, qs)] * n_da + [(None, True, 1.0)] * n_da + [(None, False, 1.0)] * n_da
            + [(0, True, qs)] * gq_heads + [(1, True, 1.0)] * gq_kv + [(None, False, 1.0)] * gq_kv)
    cos2, sin2 = _rope_tables(L, Lc)
    dft_lat = _dft_matrices(L)
    dft_ctx = _dft_matrices(Lc)

    xs = jnp.concatenate([x, ctx], axis=1)
    cond = jnp.concatenate([c, c_ctx[None]], axis=0)
    cond = _pad_to(jax.nn.silu(cond), 0, V7X_SUBLANES)

    mods_all = []
    for layer in range(depth):
        m = _matmul(_matmul(cond, ada_down[layer], name="ada_down"), ada_up[layer], name="ada_up")
        mods_all.append((m[:B + 1] + ada_b[layer]).reshape(B + 1, N_MOD, D))

    for layer in range(depth):
        last = layer == depth - 1
        mods = mods_all[layer]

        if layer == 0:
            h = _modulate(xs, mods, L, 0, 1)
        w_in_l = _pad_to(w_in[layer], 1, n_pad).astype(BF16)
        p = _matmul(h.reshape(B * Lt, D), w_in_l, name="in_proj").reshape(B, Lt, n_pad)

        a = _qk_prep(p, cos2, sin2, gq_qk_norm[layer], plan)
        da_lat = _attention(a, n_da, o_daq // 128, o_dak // 128, 1, o_dav // da_v, 2, da_v, (0, L), (0, Lt))
        da_ctx = _attention(a, n_da, o_daq // 128, o_dak // 128, 1, o_dav // da_v, 2, da_v, (L, Lc), (L, Lc))
        gq_lat = _attention(a, gq_heads, o_gqq // 128, o_gqk // 128, gq_rep, o_gqv // 128, gq_rep, HEAD_DIM,
                            (0, L), (0, Lt))
        gq_ctx = _attention(a, gq_heads, o_gqq // 128, o_gqk // 128, gq_rep, o_gqv // 128, gq_rep, HEAD_DIM,
                            (L, Lc), (L, Lc))
        lam_init = 0.8 - 0.6 * math.exp(-0.3 * layer)
        lp = da_lambda[layer]
        lam = jnp.exp(jnp.sum(lp[0] * lp[1])) - jnp.exp(jnp.sum(lp[2] * lp[3])) + lam_init
        da_o = jnp.concatenate([da_lat, da_ctx], axis=1).reshape(B, Lt, da_heads, 2, da_v)
        da = (_rms(da_o[:, :, :, 0] - lam * da_o[:, :, :, 1], da_subln[layer]) * (1.0 - lam_init))
        da = da.reshape(B, Lt, da_heads * da_v)
        gq = jnp.concatenate([gq_lat, gq_ctx], axis=1)

        u3 = p[:, :, o_hy:o_hy + 3 * hy_w]
        hy_args = (hy_conv_w[layer], hy_conv_b[layer])
        filt_args = (hy_f_w1[layer], hy_f_b1[layer], hy_f_w2[layer], hy_f_b2[layer], hy_f_w3[layer],
                     hy_sin_freq[layer], hy_w)
        hy_lat = _hyena(u3[:, :L], *hy_args, _hyena_filters(L, *filt_args), hy_bias[layer], dft_lat)
        hy_ctx = _hyena(u3[:, L:], *hy_args, _hyena_filters(Lc, *filt_args), hy_bias[layer], dft_ctx)
        hy = jnp.concatenate([hy_lat, hy_ctx], axis=1)

        g = (p[:, :, o_mlg:o_mlg + 4 * ML_HEADS] + ml_gate_b[layer]).reshape(B, Lt, 4, ML_HEADS)
        gates = jnp.transpose(g, (0, 3, 1, 2))
        hf, hb = _mlstm(p, gates, L, o_mlq // ml_d, o_mlk // ml_d, o_mlv // ml_d, ml_d, S)
        hn = _rms((hf + hb).reshape(B, Lt, ML_HEADS, ml_d), ml_norm[layer].reshape(ML_HEADS, ml_d))
        ml = jax.nn.sigmoid(p[:, :, o_mlo:o_mlo + ml_w]) * hn.reshape(B, Lt, ml_w)

        cat = jnp.concatenate([da, gq, hy, ml], axis=-1).astype(BF16)
        mix = _matmul(cat.reshape(B * Lt, D), w_out[layer].astype(BF16), name="out_proj").reshape(B, Lt, D)

        xs, hm = _resid_ln(xs, mix, mods, mods, ln_g[layer, 0], ln_b[layer, 0], L, alpha, 2, 3, 4)

        logits = _matmul(hm.reshape(B * Lt, D), _pad_to(router[layer], 1, V7X_LANES), name="router")
        aff = jax.nn.softmax(logits[:, :E].reshape(B, Lt, E), axis=-1)
        cap, cap_c = CAPACITY_FACTOR * L // E, CAPACITY_FACTOR * Lc // E
        gate_l, idx_l = lax.top_k(jnp.swapaxes(aff[:, :L], 1, 2), cap)
        gate_c, idx_c = lax.top_k(jnp.swapaxes(aff[:, L:], 1, 2), cap_c)
        idx = jnp.concatenate([idx_l, idx_c + L], axis=-1).astype(jnp.int32)
        gate = jnp.concatenate([gate_l, gate_c], axis=-1)
        R = cap + cap_c
        y = _expert_ffn(hm, idx.reshape(B * E, 1, R), gate.reshape(B * E, R, 1),
                        ex_w1[layer].astype(BF16), ex_w3[layer].astype(BF16), ex_w2[layer].astype(BF16))
        bidx = jnp.arange(B)[:, None, None]
        ffn = jnp.zeros((B, Lt, D), F32).at[bidx, idx].add(y.reshape(B, E, R, D))

        if last:
            xs, _ = _resid_ln(xs, ffn, mods, mods, ln_g[layer, 1], ln_b[layer, 1], L, alpha, 5)
        else:
            xs, h = _resid_ln(xs, ffn, mods, mods_all[layer + 1], ln_g[layer, 1], ln_b[layer, 1], L, alpha, 5, 0, 1,
                              h_dtype=BF16)
    return xs[:, :L]
```

```python
import functools
import math

import jax
import jax.numpy as jnp
import numpy as np
from jax import lax
from jax.experimental import pallas as pl
from jax.experimental.pallas import tpu as pltpu

HEAD_DIM = 128
GRID_W = 64
ROPE_THETA = 10000.0
HY_ORDER = 2
HY_EMB = 33
HY_SLOW_RATE = -math.log(1e-2) / 1.5
HY_FAST_RATE = -math.log(1e-2) / 0.3
ML_HEADS = 4
M_EMPTY = -1e30
CAPACITY_FACTOR = 2
N_MOD = 6
GQ_HEADS_PER_STEP = 2
EPS = 1e-6
LN_EPS = 1e-5

V7X_LANES = 128
V7X_SUBLANES = 8
V7X_VMEM_BYTES = 64 * 1024 * 1024
V7X_VMEM_REQUEST_CAP = 56 * 1024 * 1024
MIN_VMEM_REQUEST = 32 * 1024 * 1024

BF16 = jnp.bfloat16
F32 = jnp.float32


def _pick_tile(n, pref, mult=V7X_LANES):
    if n <= pref:
        return n
    t = (pref // mult) * mult
    while t >= mult:
        if n % t == 0:
            return t
        t -= mult
    return n


def _vmem_limit(block_bytes):
    return int(min(max(block_bytes * 5 // 4, MIN_VMEM_REQUEST), V7X_VMEM_REQUEST_CAP))


def _mm_body(a_ref, b_ref, o_ref, *acc, nk, kaxis):
    prod = jnp.dot(a_ref[...].astype(BF16), b_ref[...].astype(BF16), preferred_element_type=F32)
    if nk == 1:
        o_ref[...] = prod.astype(o_ref.dtype)
        return
    (acc_ref,) = acc
    k = pl.program_id(kaxis)

    @pl.when(k == 0)
    def _():
        acc_ref[...] = prod

    @pl.when(k > 0)
    def _():
        acc_ref[...] += prod

    @pl.when(k == nk - 1)
    def _():
        o_ref[...] = acc_ref[...].astype(o_ref.dtype)


def _matmul(a, b, out_dtype=F32, tm=1024, tn=512, tk=4096, name="matmul"):
    M, K = a.shape
    batched = b.ndim == 3
    N = b.shape[-1]
    assert b.shape[-2] == K
    tm = _pick_tile(M, tm, V7X_SUBLANES)
    tn = _pick_tile(N, tn)
    tk = _pick_tile(K, tk)
    nk = K // tk
    abytes = tm * tk * a.dtype.itemsize
    bbytes = tk * tn * b.dtype.itemsize
    obytes = tm * tn * jnp.dtype(out_dtype).itemsize
    vmem = _vmem_limit(2 * (abytes + bbytes + obytes) + (tm * tn * 4 if nk > 1 else 0)
                       + tm * tn * 4 + (tm * tk + tk * tn) * 2)
    scratch = [pltpu.VMEM((tm, tn), F32)] if nk > 1 else []
    if batched:
        G = b.shape[0]
        grid = (M // tm, G, N // tn, nk)
        in_specs = [pl.BlockSpec((tm, tk), lambda i, g, j, k: (i, k)),
                    pl.BlockSpec((None, tk, tn), lambda i, g, j, k: (g, k, j))]
        out_specs = pl.BlockSpec((None, tm, tn), lambda i, g, j, k: (g, i, j))
        out_shape = jax.ShapeDtypeStruct((G, M, N), out_dtype)
        sem = ("parallel", "parallel", "parallel", "arbitrary")
        kaxis = 3
    else:
        grid = (M // tm, N // tn, nk)
        in_specs = [pl.BlockSpec((tm, tk), lambda i, j, k: (i, k)),
                    pl.BlockSpec((tk, tn), lambda i, j, k: (k, j))]
        out_specs = pl.BlockSpec((tm, tn), lambda i, j, k: (i, j))
        out_shape = jax.ShapeDtypeStruct((M, N), out_dtype)
        sem = ("parallel", "parallel", "arbitrary")
        kaxis = 2
    return pl.pallas_call(
        functools.partial(_mm_body, nk=nk, kaxis=kaxis),
        out_shape=out_shape, grid=grid, in_specs=in_specs, out_specs=out_specs,
        scratch_shapes=scratch, name=name,
        compiler_params=pltpu.CompilerParams(dimension_semantics=sem, vmem_limit_bytes=vmem),
    )(a, b)


def _mod_index(b, i, nlat, nb):
    return jnp.where(i < nlat, b, nb)


def _modulate_body(x_ref, m_ref, h_ref, *, shift, scale):
    m = m_ref[...]
    h_ref[...] = (x_ref[...] * (1.0 + m[scale:scale + 1]) + m[shift:shift + 1]).astype(h_ref.dtype)


def _modulate(x, mods, L, shift, scale, out_dtype=BF16, tr=256):
    B, Lt, D = x.shape
    tr = _pick_tile(math.gcd(L, Lt - L), tr, V7X_SUBLANES)
    nlat = L // tr
    blk = tr * D * (4 + jnp.dtype(out_dtype).itemsize)
    return pl.pallas_call(
        functools.partial(_modulate_body, shift=shift, scale=scale),
        out_shape=jax.ShapeDtypeStruct((B, Lt, D), out_dtype),
        grid=(B, Lt // tr),
        in_specs=[pl.BlockSpec((None, tr, D), lambda b, i: (b, i, 0)),
                  pl.BlockSpec((None, N_MOD, D), lambda b, i: (_mod_index(b, i, nlat, B), 0, 0))],
        out_specs=pl.BlockSpec((None, tr, D), lambda b, i: (b, i, 0)),
        name="modulate",
        compiler_params=pltpu.CompilerParams(dimension_semantics=("parallel", "parallel"),
                                             vmem_limit_bytes=_vmem_limit(2 * blk)),
    )(x, mods)


def _resid_ln_body(x_ref, y_ref, m_ref, m2_ref, g_ref, b_ref, xo_ref, *h_ref, alpha, gate, shift, scale):
    m = m_ref[...]
    v = alpha * x_ref[...] + m[gate:gate + 1] * y_ref[...]
    mu = jnp.mean(v, axis=-1, keepdims=True)
    vc = v - mu
    var = jnp.mean(vc * vc, axis=-1, keepdims=True)
    out = vc * lax.rsqrt(var + LN_EPS) * g_ref[...] + b_ref[...]
    xo_ref[...] = out
    if h_ref:
        m2 = m2_ref[...]
        h_ref[0][...] = (out * (1.0 + m2[scale:scale + 1]) + m2[shift:shift + 1]).astype(h_ref[0].dtype)


def _resid_ln(x, y, mods, mods2, g, b, L, alpha, gate, shift=None, scale=None, h_dtype=F32, tr=128):
    B, Lt, D = x.shape
    tr = _pick_tile(math.gcd(L, Lt - L), tr, V7X_SUBLANES)
    nlat = L // tr
    emit_h = shift is not None
    row = pl.BlockSpec((None, tr, D), lambda b_, i: (b_, i, 0))
    modspec = pl.BlockSpec((None, N_MOD, D), lambda b_, i: (_mod_index(b_, i, nlat, B), 0, 0))
    vec = pl.BlockSpec((1, D), lambda b_, i: (0, 0))
    out_shape = [jax.ShapeDtypeStruct((B, Lt, D), F32)]
    out_specs = [row]
    if emit_h:
        out_shape.append(jax.ShapeDtypeStruct((B, Lt, D), h_dtype))
        out_specs.append(row)
    blk = tr * D * 4 * 4
    res = pl.pallas_call(
        functools.partial(_resid_ln_body, alpha=alpha, gate=gate, shift=shift, scale=scale),
        out_shape=out_shape, grid=(B, Lt // tr),
        in_specs=[row, row, modspec, modspec, vec, vec], out_specs=out_specs,
        name="resid_ln",
        compiler_params=pltpu.CompilerParams(dimension_semantics=("parallel", "parallel"),
                                             vmem_limit_bytes=_vmem_limit(2 * blk)),
    )(x, y, mods, mods2, g.reshape(1, D), b.reshape(1, D))
    return (res[0], res[1]) if emit_h else (res[0], None)


def _qk_prep_body(p_ref, cos_ref, sin_ref, gain_ref, o_ref, *, plan):
    cos = cos_ref[...]
    sin = sin_ref[...]
    tr = cos.shape[0]
    lane = lax.broadcasted_iota(jnp.int32, (tr, HEAD_DIM), 1)
    partner = lane ^ 1
    from_prev = pltpu.roll(lane, 1, axis=1) == partner
    for blk, (norm_row, rope, post) in enumerate(plan):
        sl = slice(blk * HEAD_DIM, (blk + 1) * HEAD_DIM)
        v = p_ref[:, sl]
        if norm_row is not None:
            ms = jnp.mean(v * v, axis=-1, keepdims=True)
            v = v * lax.rsqrt(ms + EPS) * gain_ref[norm_row:norm_row + 1, :]
        if rope:
            swapped = jnp.where(from_prev, pltpu.roll(v, 1, axis=1), pltpu.roll(v, HEAD_DIM - 1, axis=1))
            v = v * cos + swapped * sin
        if post != 1.0:
            v = v * post
        o_ref[:, sl] = v.astype(o_ref.dtype)


def _qk_prep(p, cos, sin, gains, plan, tr=256):
    B, Lt, _ = p.shape
    W = len(plan) * HEAD_DIM
    tr = _pick_tile(Lt, tr, 16)
    blk = tr * W * (4 + 2) + 2 * tr * HEAD_DIM * 4
    return pl.pallas_call(
        functools.partial(_qk_prep_body, plan=tuple(plan)),
        out_shape=jax.ShapeDtypeStruct((B, Lt, W), BF16),
        grid=(B, Lt // tr),
        in_specs=[pl.BlockSpec((None, tr, W), lambda b, i: (b, i, 0)),
                  pl.BlockSpec((tr, HEAD_DIM), lambda b, i: (i, 0)),
                  pl.BlockSpec((tr, HEAD_DIM), lambda b, i: (i, 0)),
                  pl.BlockSpec(gains.shape, lambda b, i: (0, 0))],
        out_specs=pl.BlockSpec((None, tr, W), lambda b, i: (b, i, 0)),
        name="qk_prep",
        compiler_params=pltpu.CompilerParams(dimension_semantics=("parallel", "parallel"),
                                             vmem_limit_bytes=_vmem_limit(2 * blk)),
    )(p, cos, sin, gains)


def _softmax_pv(q, k, v):
    s = lax.dot_general(q, k, (((1,), (1,)), ((), ())), preferred_element_type=F32)
    m = jnp.max(s, axis=-1, keepdims=True)
    e = jnp.exp(s - m)
    l = jnp.sum(e, axis=-1, keepdims=True)
    return jnp.dot(e.astype(BF16), v, preferred_element_type=F32) / l


def _key_ranges(fn, nlat, L):
    tile = pl.program_id(2)

    @pl.when(tile < nlat)
    def _():
        fn(0)

    @pl.when(tile >= nlat)
    def _():
        fn(L)


def _da_attn_body(lam_ref, q_ref, k_ref, v_ref, g_ref, o_ref, *, nlat, L):
    def run(lo):
        v = v_ref[lo:, :]
        o0 = _softmax_pv(q_ref[:, :HEAD_DIM], k_ref[lo:, :HEAD_DIM], v)
        o1 = _softmax_pv(q_ref[:, HEAD_DIM:], k_ref[lo:, HEAD_DIM:], v)
        d = o0 - lam_ref[0, 0] * o1
        ms = jnp.mean(d * d, axis=-1, keepdims=True)
        o_ref[...] = (d * lax.rsqrt(ms + EPS) * g_ref[...]).astype(o_ref.dtype)

    _key_ranges(run, nlat, L)


def _gq_attn_body(q_ref, k_ref, v_ref, o_ref, *, hp, nlat, L):
    def run(lo):
        k = k_ref[lo:, :]
        v = v_ref[lo:, :]
        for j in range(hp):
            sl = slice(j * HEAD_DIM, (j + 1) * HEAD_DIM)
            o_ref[:, sl] = _softmax_pv(q_ref[:, sl], k, v).astype(o_ref.dtype)

    _key_ranges(run, nlat, L)


def _attention(body, a, L, groups, qw, kw, vw, ow, q_col, k_col, v_col, kv_div, scalars=(), vectors=(),
               tq=256, name="attention"):
    B, Lt, _ = a.shape
    tq = _pick_tile(math.gcd(L, Lt - L), tq, 16)
    in_specs = [pl.BlockSpec(memory_space=pltpu.SMEM) for _ in scalars]
    in_specs += [pl.BlockSpec((None, tq, qw), lambda b, h, i: (b, i, q_col + h)),
                 pl.BlockSpec((None, Lt, kw), lambda b, h, i: (b, 0, k_col + h // kv_div)),
                 pl.BlockSpec((None, Lt, vw), lambda b, h, i: (b, 0, v_col + h // kv_div))]
    in_specs += [pl.BlockSpec(vec.shape, lambda b, h, i: (0, 0)) for vec in vectors]
    nprob = max(qw // HEAD_DIM, 1)
    blk = tq * qw * 2 + Lt * (kw + vw) * 2 + tq * ow * 2
    inter = nprob * tq * Lt * (4 + 4 + 2)
    return pl.pallas_call(
        functools.partial(body, nlat=L // tq, L=L),
        out_shape=jax.ShapeDtypeStruct((B, Lt, groups * ow), BF16),
        grid=(B, groups, Lt // tq),
        in_specs=in_specs,
        out_specs=pl.BlockSpec((None, tq, ow), lambda b, h, i: (b, i, h)),
        name=name,
        compiler_params=pltpu.CompilerParams(dimension_semantics=("parallel", "parallel", "parallel"),
                                             vmem_limit_bytes=_vmem_limit(2 * blk + inter)),
    )(*scalars, a, a, a, *vectors)


def _mm_multi_body(*refs, ks):
    n = len(ks)
    b_ref, o_ref = refs[n], refs[n + 1]
    acc, off = None, 0
    for a_ref, kk in zip(refs[:n], ks):
        part = jnp.dot(a_ref[...].astype(BF16), b_ref[off:off + kk, :], preferred_element_type=F32)
        acc = part if acc is None else acc + part
        off += kk
    o_ref[...] = acc.astype(o_ref.dtype)


def _matmul_multi(a_list, b, out_dtype=F32, tm=1024, tn=512, name="matmul_multi"):
    M = a_list[0].shape[0]
    ks = tuple(a.shape[1] for a in a_list)
    K, N = b.shape
    assert sum(ks) == K
    tm = _pick_tile(M, tm, V7X_SUBLANES)
    tn = _pick_tile(N, tn)
    blk = sum(tm * kk * a.dtype.itemsize for a, kk in zip(a_list, ks)) + K * tn * b.dtype.itemsize + tm * tn * 4
    return pl.pallas_call(
        functools.partial(_mm_multi_body, ks=ks),
        out_shape=jax.ShapeDtypeStruct((M, N), out_dtype),
        grid=(M // tm, N // tn),
        in_specs=[pl.BlockSpec((tm, kk), lambda i, j: (i, 0)) for kk in ks]
        + [pl.BlockSpec((K, tn), lambda i, j: (0, j))],
        out_specs=pl.BlockSpec((tm, tn), lambda i, j: (i, j)),
        name=name,
        compiler_params=pltpu.CompilerParams(dimension_semantics=("parallel", "parallel"),
                                             vmem_limit_bytes=_vmem_limit(2 * blk + 2 * tm * tn * 4)),
    )(*a_list, b)


def _ml_post_body(hf_ref, hb_ref, o_ref, g_ref, out_ref):
    hs = hf_ref[...] + hb_ref[...]
    ms = jnp.mean(hs * hs, axis=-1, keepdims=True)
    hn = hs * lax.rsqrt(ms + EPS) * g_ref[...]
    out_ref[...] = (jax.nn.sigmoid(o_ref[...]) * hn).astype(out_ref.dtype)


def _ml_post(hf, hb, p, o_col, gains, tr=256):
    B, Lt, W = hf.shape
    H, _, d = gains.shape
    tr = _pick_tile(Lt, tr, 16)
    head = pl.BlockSpec((None, tr, d), lambda b, i, h: (b, i, h))
    blk = tr * d * (4 * 3 + 2)
    return pl.pallas_call(
        _ml_post_body,
        out_shape=jax.ShapeDtypeStruct((B, Lt, W), BF16),
        grid=(B, Lt // tr, H),
        in_specs=[head, head, pl.BlockSpec((None, tr, d), lambda b, i, h: (b, i, o_col + h)),
                  pl.BlockSpec((None, 1, d), lambda b, i, h: (h, 0, 0))],
        out_specs=head,
        name="ml_post",
        compiler_params=pltpu.CompilerParams(dimension_semantics=("parallel", "parallel", "parallel"),
                                             vmem_limit_bytes=_vmem_limit(2 * blk)),
    )(hf, hb, p, gains)


def _mlstm_dir(q, k, v, ig, fpre, c_ref, n_ref, m_ref, reverse):
    S = q.shape[0]
    lf = jax.nn.log_sigmoid(fpre)
    r = lax.broadcasted_iota(jnp.int32, (S, S), 0)
    c = lax.broadcasted_iota(jnp.int32, (S, S), 1)
    eye = r == c
    seen = (c >= r) if reverse else (c <= r)
    seen_t = (r >= c) if reverse else (r <= c)
    zero = jnp.zeros((S, S), F32)
    lf_row = jnp.sum(jnp.where(eye, lf, zero), axis=0, keepdims=True)
    ig_row = jnp.sum(jnp.where(eye, ig, zero), axis=0, keepdims=True)
    b_col = jnp.sum(jnp.where(seen, lf_row, zero), axis=1, keepdims=True)
    b_row = jnp.sum(jnp.where(seen_t, lf, zero), axis=0, keepdims=True)
    total = jnp.sum(lf, axis=0, keepdims=True)
    m_prev = m_ref[...]
    g_row = total - b_row + ig_row
    g_col = total - b_col + ig
    m_end = jnp.maximum(total + m_prev, jnp.max(g_row, axis=1, keepdims=True))
    w_end = jnp.exp(g_col - m_end)
    decay = jnp.exp(total + m_prev - m_end)
    dlog = jnp.where(seen, b_col - b_row + ig_row, -jnp.inf)
    inter = b_col + m_prev
    m_row = jnp.maximum(inter, jnp.max(dlog, axis=1, keepdims=True))
    w_inter = jnp.exp(inter - m_row)
    qb, kb, vb = q.astype(BF16), k.astype(BF16), v.astype(BF16)
    c_old = c_ref[...]
    n_old = n_ref[...]
    qk = lax.dot_general(qb, kb, (((1,), (1,)), ((), ())), preferred_element_type=F32)
    s = qk * jnp.exp(dlog - m_row)
    cq = lax.dot_general(qb, c_old.astype(BF16), (((1,), (1,)), ((), ())), preferred_element_type=F32)
    num = w_inter * cq + jnp.dot(s.astype(BF16), vb, preferred_element_type=F32)
    den = w_inter * jnp.sum(q * n_old, axis=1, keepdims=True) + jnp.sum(s, axis=1, keepdims=True)
    h = num / jnp.maximum(jnp.abs(den), jnp.exp(-m_row))
    vw_t = (v * w_end).T.astype(BF16)
    c_ref[...] = decay * c_old + jnp.dot(vw_t, kb, preferred_element_type=F32)
    n_ref[...] = decay * n_old + jnp.sum(k * w_end, axis=0, keepdims=True)
    m_ref[...] = m_end
    return h


def _mlstm_body(qf_ref, kf_ref, vf_ref, gf_ref, qb_ref, kb_ref, vb_ref, gb_ref, hf_ref, hb_ref,
                cf, nf, mf, cb, nb, mb, *, qscale):
    @pl.when(pl.program_id(2) == 0)
    def _():
        for c_ref, n_ref, m_ref in ((cf, nf, mf), (cb, nb, mb)):
            c_ref[...] = jnp.zeros_like(c_ref)
            n_ref[...] = jnp.zeros_like(n_ref)
            m_ref[...] = jnp.full_like(m_ref, M_EMPTY)

    gf = gf_ref[...]
    gb = gb_ref[...]
    hf_ref[...] = _mlstm_dir(qf_ref[...] * qscale, kf_ref[...], vf_ref[...], gf[:, 0:1], gf[:, 1:2],
                             cf, nf, mf, False)
    hb_ref[...] = _mlstm_dir(qb_ref[...] * qscale, kb_ref[...], vb_ref[...], gb[:, 2:3], gb[:, 3:4],
                             cb, nb, mb, True)


def _mlstm(p, gates, L, q_col, k_col, v_col, d, S):
    B, Lt, _ = p.shape
    H = gates.shape[1]
    nch = Lt // S
    nlat = L // S

    def fwd(t):
        return jnp.where(t < nch - nlat, nlat + t, t - (nch - nlat))

    def bwd(t):
        return nch - 1 - t

    def seg(col, order):
        return pl.BlockSpec((None, S, d), lambda b, h, t: (b, order(t), col + h))

    def gate(order):
        return pl.BlockSpec((None, None, S, 4), lambda b, h, t: (b, h, order(t), 0))

    out_f = pl.BlockSpec((None, S, d), lambda b, h, t: (b, fwd(t), h))
    out_b = pl.BlockSpec((None, S, d), lambda b, h, t: (b, bwd(t), h))
    state = [pltpu.VMEM((d, d), F32), pltpu.VMEM((1, d), F32), pltpu.VMEM((1, 1), F32)]
    blk = 8 * S * d * 4 + 2 * S * V7X_LANES * 4
    return pl.pallas_call(
        functools.partial(_mlstm_body, qscale=d ** -0.5),
        out_shape=[jax.ShapeDtypeStruct((B, Lt, H * d), F32)] * 2,
        grid=(B, H, nch),
        in_specs=[seg(q_col, fwd), seg(k_col, fwd), seg(v_col, fwd), gate(fwd),
                  seg(q_col, bwd), seg(k_col, bwd), seg(v_col, bwd), gate(bwd)],
        out_specs=[out_f, out_b],
        scratch_shapes=state + state,
        name="mlstm",
        compiler_params=pltpu.CompilerParams(dimension_semantics=("parallel", "parallel", "arbitrary"),
                                             vmem_limit_bytes=_vmem_limit(2 * blk + 16 * S * S * 4 + 4 * d * d * 4)),
    )(p, p, p, gates, p, p, p, gates)


def _row_in(src_hbm, dst_ref, sem, b, src_row, dst_row):
    return pltpu.make_async_copy(src_hbm.at[b, pl.ds(src_row, 1), :], dst_ref.at[pl.ds(dst_row, 1), :], sem)


def _row_out(src_ref, dst_hbm, sem, b, src_row, dst_row):
    return pltpu.make_async_copy(src_ref.at[pl.ds(src_row, 1), :], dst_hbm.at[b, pl.ds(dst_row, 1), :], sem)


def _expert_ffn_body(idx_ref, gate_ref, h_hbm, acc_in, w1_ref, w3_ref, w2_ref, acc_hbm, xs_ref, ys_ref,
                     sem_x, sem_y, sem_s, *, rows, tn, unroll):
    del acc_in
    b = pl.program_id(1)

    def gather(r, carry):
        row = idx_ref[0, 0, r]
        _row_in(h_hbm, xs_ref, sem_x, b, row, r).start()
        _row_in(acc_hbm, ys_ref, sem_y, b, row, r).start()
        return carry

    lax.fori_loop(0, rows, gather, 0, unroll=unroll)

    def wait_x(r, carry):
        _row_in(h_hbm, xs_ref, sem_x, b, 0, r).wait()
        return carry

    lax.fori_loop(0, rows, wait_x, 0, unroll=unroll)
    xs = xs_ref[...].astype(BF16)
    a = jnp.dot(xs, w1_ref[...], preferred_element_type=F32)
    g = jnp.dot(xs, w3_ref[...], preferred_element_type=F32)
    hid = (jax.nn.silu(a) * g).astype(BF16)

    def wait_y(r, carry):
        _row_in(acc_hbm, ys_ref, sem_y, b, 0, r).wait()
        return carry

    lax.fori_loop(0, rows, wait_y, 0, unroll=unroll)
    gate = gate_ref[...]
    for j in range(ys_ref.shape[1] // tn):
        sl = slice(j * tn, (j + 1) * tn)
        ys_ref[:, sl] += jnp.dot(hid, w2_ref[:, sl], preferred_element_type=F32) * gate

    def scatter(r, carry):
        _row_out(ys_ref, acc_hbm, sem_s, b, r, idx_ref[0, 0, r]).start()
        return carry

    lax.fori_loop(0, rows, scatter, 0, unroll=unroll)

    def wait_s(r, carry):
        _row_out(ys_ref, acc_hbm, sem_s, b, r, 0).wait()
        return carry

    lax.fori_loop(0, rows, wait_s, 0, unroll=unroll)


def _expert_ffn(hm, acc, idx, gate, w1, w3, w2, tn=1024, unroll=8):
    B, _, D = hm.shape
    E, _, F = w1.shape
    R = idx.shape[-1]
    tn = _pick_tile(D, tn)
    unroll = unroll if R % unroll == 0 else 1
    blk = 3 * D * F * 2 + R * V7X_LANES * 4
    return pl.pallas_call(
        functools.partial(_expert_ffn_body, rows=R, tn=tn, unroll=unroll),
        out_shape=jax.ShapeDtypeStruct(acc.shape, F32),
        grid=(E, B),
        in_specs=[pl.BlockSpec((1, 1, R), lambda e, b: (b * E + e, 0, 0), memory_space=pltpu.SMEM),
                  pl.BlockSpec((None, R, 1), lambda e, b: (b * E + e, 0, 0)),
                  pl.BlockSpec(memory_space=pl.ANY),
                  pl.BlockSpec(memory_space=pl.ANY),
                  pl.BlockSpec((None, D, F), lambda e, b: (e, 0, 0)),
                  pl.BlockSpec((None, D, F), lambda e, b: (e, 0, 0)),
                  pl.BlockSpec((None, F, D), lambda e, b: (e, 0, 0))],
        out_specs=pl.BlockSpec(memory_space=pl.ANY),
        input_output_aliases={3: 0},
        scratch_shapes=[pltpu.VMEM((R, D), F32), pltpu.VMEM((R, D), F32),
                        pltpu.SemaphoreType.DMA(()), pltpu.SemaphoreType.DMA(()), pltpu.SemaphoreType.DMA(())],
        name="expert_ffn",
        compiler_params=pltpu.CompilerParams(dimension_semantics=("arbitrary", "arbitrary"),
                                             vmem_limit_bytes=_vmem_limit(2 * blk + R * D * 10 + R * tn * 4
                                                                          + R * F * 10)),
    )(idx, gate, hm, acc, w1, w3, w2)


def _rope_tables(L, Lc):
    rows = L // GRID_W
    row = jnp.repeat(jnp.arange(rows, dtype=F32), GRID_W)
    col = jnp.tile(jnp.arange(GRID_W, dtype=F32), rows)
    half = HEAD_DIM // 2
    inv = ROPE_THETA ** (-jnp.arange(0, half, 2, dtype=F32) / half)
    ang = jnp.concatenate([row[:, None] * inv, col[:, None] * inv], axis=-1)
    cos, sin = jnp.cos(ang), jnp.sin(ang)
    cos2 = jnp.repeat(cos, 2, axis=-1)
    sin2 = jnp.stack([-sin, sin], axis=-1).reshape(L, HEAD_DIM)
    cos2 = jnp.concatenate([cos2, jnp.ones((Lc, HEAD_DIM), F32)], axis=0)
    sin2 = jnp.concatenate([sin2, jnp.zeros((Lc, HEAD_DIM), F32)], axis=0)
    return cos2, sin2


def _dft_matrices(L):
    N = 2 * L
    k = jnp.arange(L, dtype=jnp.int32)
    n = jnp.arange(L, dtype=jnp.int32)
    ang = (2.0 * math.pi / N) * ((k[:, None] * n[None, :]) % N).astype(F32)
    cos, sin = jnp.cos(ang), jnp.sin(ang)
    nyq = jnp.where(n % 2 == 0, 1.0, -1.0).astype(F32)[None, :]
    first = (k == 0)[:, None]
    fwd = jnp.concatenate([cos, jnp.where(first, nyq, -sin)], axis=0)
    inv_re = jnp.where(first, 1.0, 2.0) * cos / N
    inv_im = jnp.where(first, nyq / N, -2.0 * sin / N)
    inv = jnp.concatenate([inv_re, inv_im], axis=0).T
    return fwd.astype(BF16), inv.astype(BF16)


def _pad_to(a, axis, size):
    pad = [(0, 0)] * a.ndim
    pad[axis] = (0, size - a.shape[axis])
    return jnp.pad(a, pad)


def _hyena_filters(L, w1, b1, w2, b2, w3, sin_freq, width):
    bands = (HY_EMB - 1) // 2
    t = jnp.linspace(0.0, 1.0, L, dtype=F32)[:, None]
    w = (2.0 * math.pi / L) * jnp.arange(L, dtype=F32)[:, None]
    f = jnp.linspace(1e-4, bands - 1, bands, dtype=F32)[None, :]
    z = jnp.concatenate([t, jnp.cos(f * w), -jnp.sin(f * w)], axis=-1)
    ffn = w1.shape[1]
    a = _matmul(_pad_to(z, 1, V7X_LANES), _pad_to(_pad_to(w1, 0, V7X_LANES), 1, V7X_LANES), name="hy_f1")
    a = jnp.sin(sin_freq[0] * (a[:, :ffn] + b1))
    a = _matmul(_pad_to(a, 1, V7X_LANES), _pad_to(_pad_to(w2, 0, V7X_LANES), 1, V7X_LANES), name="hy_f2")
    a = jnp.sin(sin_freq[1] * (a[:, :ffn] + b2))
    h = _matmul(_pad_to(a, 1, V7X_LANES), _pad_to(w3, 0, V7X_LANES), name="hy_f3")
    h = h.reshape(L, HY_ORDER, 2, width)
    rates = jnp.linspace(HY_SLOW_RATE, HY_FAST_RATE, width, dtype=F32)
    h = h * jnp.exp(-t[:, :, None, None] * rates)
    return h * lax.rsqrt(jnp.sum(jnp.square(h), axis=(0, 2), keepdims=True) + EPS)


def _hyena(u3, conv_w, conv_b, filt, bias, dft):
    B, L, C3 = u3.shape
    C = C3 // 3
    fwd, inv = dft
    up = jnp.pad(u3, ((0, 0), (1, 1), (0, 0)))
    u = conv_w[0] * up[:, :-2] + conv_w[1] * up[:, 1:-1] + conv_w[2] * up[:, 2:] + conv_b
    first = (jnp.arange(L) == 0)[:, None]
    hf = filt[:, :, 0]
    hb = jnp.where(first[:, :, None], 0.0, filt[:, :, 1])
    hsum = jnp.moveaxis(hf + hb, 1, 0)
    hdif = jnp.moveaxis(hf - hb, 1, 0)
    sign = jnp.where(jnp.arange(L) % 2 == 0, 1.0, -1.0).astype(F32)[None, :, None]
    Kr_all = _matmul(fwd[:L], hsum.astype(BF16), name="hy_dft_filter")
    Ki_all = _matmul(fwd[L:], hdif.astype(BF16), name="hy_dft_filter")
    nyq = jnp.sum(hsum * sign, axis=1, keepdims=True)
    Ki_all = jnp.where(first, nyq, Ki_all)
    zb = u[..., :C].astype(BF16)
    z = u
    for o in range(HY_ORDER):
        yr, yi = _dft_fwd_mul(fwd, zb, Kr_all[o], Ki_all[o])
        last = o == HY_ORDER - 1
        res = _dft_inv_gate(inv, yr, yi, u, (1 + o) * C, z, 0, bias[o].reshape(1, C), emit_f32=not last)
        if last:
            return res[0]
        z, zb = res


def _dft_fwd_mul_body(fr_ref, fi_ref, z_ref, kr_ref, ki_ref, yr_ref, yi_ref):
    z = z_ref[...]
    ur = jnp.dot(fr_ref[...], z, preferred_element_type=F32)
    ui = jnp.dot(fi_ref[...], z, preferred_element_type=F32)
    kr, ki = kr_ref[...], ki_ref[...]
    tm = ur.shape[0]
    row = lax.broadcasted_iota(jnp.int32, (tm, 1), 0) + pl.program_id(0) * tm
    first = row == 0
    ii = ui * ki
    yr_ref[...] = (ur * kr - jnp.where(first, 0.0, ii)).astype(yr_ref.dtype)
    yi_ref[...] = jnp.where(first, ii, ur * ki + ui * kr).astype(yi_ref.dtype)


def _dft_fwd_mul(fwd, zb, kr, ki, tm=512, tn=512):
    B, L, C = zb.shape
    tm = _pick_tile(L, tm)
    tn = _pick_tile(C, tn)
    nlo = L // tm
    blk = 2 * tm * L * 2 + L * tn * 2 + 2 * tm * tn * 4 + 2 * tm * tn * 2
    out = jax.ShapeDtypeStruct((B, L, C), BF16)
    return pl.pallas_call(
        _dft_fwd_mul_body,
        out_shape=[out, out],
        grid=(nlo, B, C // tn),
        in_specs=[pl.BlockSpec((tm, L), lambda i, b, j: (i, 0)),
                  pl.BlockSpec((tm, L), lambda i, b, j: (nlo + i, 0)),
                  pl.BlockSpec((None, L, tn), lambda i, b, j: (b, 0, j)),
                  pl.BlockSpec((tm, tn), lambda i, b, j: (i, j)),
                  pl.BlockSpec((tm, tn), lambda i, b, j: (i, j))],
        out_specs=[pl.BlockSpec((None, tm, tn), lambda i, b, j: (b, i, j))] * 2,
        name="hy_dft_fwd",
        compiler_params=pltpu.CompilerParams(dimension_semantics=("parallel", "parallel", "parallel"),
                                             vmem_limit_bytes=_vmem_limit(2 * blk + 4 * tm * tn * 4)),
    )(fwd, fwd, zb, kr, ki)


def _dft_inv_gate_body(ir_ref, ii_ref, yr_ref, yi_ref, g_ref, z_ref, b_ref, *outs, emit_f32):
    conv = (jnp.dot(ir_ref[...], yr_ref[...], preferred_element_type=F32)
            + jnp.dot(ii_ref[...], yi_ref[...], preferred_element_type=F32))
    zn = g_ref[...] * (conv + b_ref[...] * z_ref[...])
    if emit_f32:
        outs[0][...] = zn
    outs[-1][...] = zn.astype(BF16)


def _dft_inv_gate(inv, yr, yi, u, gate_col, z, z_col, bias, emit_f32, tm=512, tn=512):
    B, L, C = yr.shape
    tm = _pick_tile(L, tm)
    tn = _pick_tile(C, tn)
    gj, zj = gate_col // tn, z_col // tn
    tile = pl.BlockSpec((None, tm, tn), lambda i, b, j: (b, i, j))
    spec = pl.BlockSpec((None, L, tn), lambda i, b, j: (b, 0, j))
    blk = 2 * tm * L * 2 + 2 * L * tn * 2 + 3 * tm * tn * 4 + tm * tn * 2
    out_shape = [jax.ShapeDtypeStruct((B, L, C), BF16)]
    if emit_f32:
        out_shape.insert(0, jax.ShapeDtypeStruct((B, L, C), F32))
    return pl.pallas_call(
        functools.partial(_dft_inv_gate_body, emit_f32=emit_f32),
        out_shape=out_shape,
        grid=(L // tm, B, C // tn),
        in_specs=[pl.BlockSpec((tm, L), lambda i, b, j: (i, 0)),
                  pl.BlockSpec((tm, L), lambda i, b, j: (i, 1)),
                  spec, spec,
                  pl.BlockSpec((None, tm, tn), lambda i, b, j: (b, i, gj + j)),
                  pl.BlockSpec((None, tm, tn), lambda i, b, j: (b, i, zj + j)),
                  pl.BlockSpec((1, tn), lambda i, b, j: (0, j))],
        out_specs=[tile] * len(out_shape),
        name="hy_dft_inv",
        compiler_params=pltpu.CompilerParams(dimension_semantics=("parallel", "parallel", "parallel"),
                                             vmem_limit_bytes=_vmem_limit(2 * blk + 2 * tm * tn * 4)),
    )(inv, inv, yr, yi, u, z, bias)


def kernel(x, c, ctx, c_ctx, ada_down, ada_up, ada_b, w_in, w_out, da_lambda, da_subln, gq_qk_norm,
           hy_conv_w, hy_conv_b, hy_f_w1, hy_f_b1, hy_f_w2, hy_f_b2, hy_f_w3, hy_sin_freq, hy_bias,
           ml_gate_b, ml_norm, ln_g, ln_b, router, ex_w1, ex_w3, ex_w2):
    B, L, D = x.shape
    Lc = ctx.shape[1]
    Lt = L + Lc
    depth = w_in.shape[0]
    E = router.shape[-1]
    alpha = (2 * depth) ** 0.25
    da_heads = D // (8 * HEAD_DIM)
    da_v = 2 * HEAD_DIM
    gq_heads = D // (4 * HEAD_DIM)
    gq_kv = gq_heads // 4
    gq_rep = gq_heads // gq_kv
    hy_w = D // 4
    ml_w = D - da_heads * da_v - gq_heads * HEAD_DIM - hy_w
    ml_d = ml_w // ML_HEADS
    widths = [da_heads * 2 * HEAD_DIM, da_heads * 2 * HEAD_DIM, da_heads * da_v, gq_heads * HEAD_DIM,
              gq_kv * HEAD_DIM, gq_kv * HEAD_DIM, hy_w, hy_w, hy_w, ml_w, ml_w, ml_w, ml_w, 4 * ML_HEADS]
    offs = np.concatenate([[0], np.cumsum(widths)]).astype(int)
    o_daq, o_dak, o_dav, o_gqq, o_gqk, o_gqv, o_hy, _, _, o_mlq, o_mlk, o_mlv, o_mlo, o_mlg, n_in = offs
    n_pad = -(-n_in // 512) * 512
    att_w = o_hy
    S = min(256, Lc)
    assert L % S == 0 and Lc % S == 0 and L % Lc == 0

    qs = HEAD_DIM ** -0.5
    n_da = 2 * da_heads
    plan = ([(None, True, qs)] * n_da + [(None, True, 1.0)] * n_da + [(None, False, 1.0)] * n_da
            + [(0, True, qs)] * gq_heads + [(1, True, 1.0)] * gq_kv + [(None, False, 1.0)] * gq_kv)
    cos2, sin2 = _rope_tables(L, Lc)
    dft_lat = _dft_matrices(L)
    dft_ctx = _dft_matrices(Lc)

    xs = jnp.concatenate([x, ctx], axis=1)
    cond = jnp.concatenate([c, c_ctx[None]], axis=0)
    cond = _pad_to(jax.nn.silu(cond), 0, V7X_SUBLANES)

    mods_all = []
    for layer in range(depth):
        m = _matmul(_matmul(cond, ada_down[layer], name="ada_down"), ada_up[layer], name="ada_up")
        mods_all.append((m[:B + 1] + ada_b[layer]).reshape(B + 1, N_MOD, D))

    for layer in range(depth):
        last = layer == depth - 1
        mods = mods_all[layer]

        if layer == 0:
            h = _modulate(xs, mods, L, 0, 1)
        w_in_l = _pad_to(w_in[layer], 1, n_pad).astype(BF16)
        p = _matmul(h.reshape(B * Lt, D), w_in_l, name="in_proj").reshape(B, Lt, n_pad)

        a = _qk_prep(p, cos2, sin2, gq_qk_norm[layer], plan)
        lam_init = 0.8 - 0.6 * math.exp(-0.3 * layer)
        lp = da_lambda[layer]
        lam = jnp.exp(jnp.sum(lp[0] * lp[1])) - jnp.exp(jnp.sum(lp[2] * lp[3])) + lam_init
        da = _attention(_da_attn_body, a, L, groups=da_heads, qw=da_v, kw=da_v, vw=da_v, ow=da_v,
                        q_col=o_daq // da_v, k_col=o_dak // da_v, v_col=o_dav // da_v, kv_div=1,
                        scalars=(lam.reshape(1, 1),),
                        vectors=((da_subln[layer] * (1.0 - lam_init)).reshape(1, da_v),), name="da_attention")
        gq_w = GQ_HEADS_PER_STEP * HEAD_DIM
        gq = _attention(functools.partial(_gq_attn_body, hp=GQ_HEADS_PER_STEP), a, L,
                        groups=gq_heads // GQ_HEADS_PER_STEP, qw=gq_w, kw=HEAD_DIM, vw=HEAD_DIM, ow=gq_w,
                        q_col=o_gqq // gq_w, k_col=o_gqk // HEAD_DIM, v_col=o_gqv // HEAD_DIM,
                        kv_div=gq_rep // GQ_HEADS_PER_STEP, name="gq_attention")

        u3 = p[:, :, o_hy:o_hy + 3 * hy_w]
        hy_args = (hy_conv_w[layer], hy_conv_b[layer])
        filt_args = (hy_f_w1[layer], hy_f_b1[layer], hy_f_w2[layer], hy_f_b2[layer], hy_f_w3[layer],
                     hy_sin_freq[layer], hy_w)
        hy_lat = _hyena(u3[:, :L], *hy_args, _hyena_filters(L, *filt_args), hy_bias[layer], dft_lat)
        hy_ctx = _hyena(u3[:, L:], *hy_args, _hyena_filters(Lc, *filt_args), hy_bias[layer], dft_ctx)
        hy = jnp.concatenate([hy_lat, hy_ctx], axis=1)

        g = (p[:, :, o_mlg:o_mlg + 4 * ML_HEADS] + ml_gate_b[layer]).reshape(B, Lt, 4, ML_HEADS)
        gates = jnp.transpose(g, (0, 3, 1, 2))
        hf, hb = _mlstm(p, gates, L, o_mlq // ml_d, o_mlk // ml_d, o_mlv // ml_d, ml_d, S)
        ml = _ml_post(hf, hb, p, o_mlo // ml_d, ml_norm[layer].reshape(ML_HEADS, 1, ml_d))

        slabs = [s.reshape(B * Lt, s.shape[-1]) for s in (da, gq, hy, ml)]
        mix = _matmul_multi(slabs, w_out[layer].astype(BF16), name="out_proj").reshape(B, Lt, D)

        xs, hm = _resid_ln(xs, mix, mods, mods, ln_g[layer, 0], ln_b[layer, 0], L, alpha, 2, 3, 4)

        logits = _matmul(hm.reshape(B * Lt, D), _pad_to(router[layer], 1, V7X_LANES), name="router")
        aff = jax.nn.softmax(logits[:, :E].reshape(B, Lt, E), axis=-1)
        cap, cap_c = CAPACITY_FACTOR * L // E, CAPACITY_FACTOR * Lc // E
        gate_l, idx_l = lax.top_k(jnp.swapaxes(aff[:, :L], 1, 2), cap)
        gate_c, idx_c = lax.top_k(jnp.swapaxes(aff[:, L:], 1, 2), cap_c)
        idx = jnp.concatenate([idx_l, idx_c + L], axis=-1).astype(jnp.int32)
        gate = jnp.concatenate([gate_l, gate_c], axis=-1)
        R = cap + cap_c
        ffn = _expert_ffn(hm, jnp.zeros((B, Lt, D), F32), idx.reshape(B * E, 1, R), gate.reshape(B * E, R, 1),
                          ex_w1[layer].astype(BF16), ex_w3[layer].astype(BF16), ex_w2[layer].astype(BF16))

        if last:
            xs, _ = _resid_ln(xs, ffn, mods, mods, ln_g[layer, 1], ln_b[layer, 1], L, alpha, 5)
        else:
            xs, h = _resid_ln(xs, ffn, mods, mods_all[layer + 1], ln_g[layer, 1], ln_b[layer, 1], L, alpha, 5, 0, 1,
                              h_dtype=BF16)
    return xs[:, :L]
```

```python
import functools
import math

import jax
import jax.numpy as jnp
import numpy as np
from jax import lax
from jax.experimental import pallas as pl
from jax.experimental.pallas import tpu as pltpu

HEAD_DIM = 128
GRID_W = 64
ROPE_THETA = 10000.0
HY_ORDER = 2
HY_EMB = 33
HY_SLOW_RATE = -math.log(1e-2) / 1.5
HY_FAST_RATE = -math.log(1e-2) / 0.3
ML_HEADS = 4
M_EMPTY = -1e30
CAPACITY_FACTOR = 2
N_MOD = 6
GQ_HEADS_PER_STEP = 2
ATTN_KEY_CHUNKS = 2
ATTN_MIN_CHUNKED_KEYS = 1024
DFT_FINE = 64
ROUTE_DIGIT = 64
EPS = 1e-6
LN_EPS = 1e-5

V7X_LANES = 128
V7X_SUBLANES = 8
V7X_VMEM_BYTES = 64 * 1024 * 1024
V7X_VMEM_REQUEST_CAP = 56 * 1024 * 1024
MIN_VMEM_REQUEST = 32 * 1024 * 1024

BF16 = jnp.bfloat16
F32 = jnp.float32


def _pick_tile(n, pref, mult=V7X_LANES):
    if n <= pref:
        return n
    t = (pref // mult) * mult
    while t >= mult:
        if n % t == 0:
            return t
        t -= mult
    return n


def _vmem_limit(block_bytes):
    return int(min(max(block_bytes * 5 // 4, MIN_VMEM_REQUEST), V7X_VMEM_REQUEST_CAP))


def _mm_body(a_ref, b_ref, o_ref, *acc, nk, kaxis):
    prod = jnp.dot(a_ref[...].astype(BF16), b_ref[...].astype(BF16), preferred_element_type=F32)
    if nk == 1:
        o_ref[...] = prod.astype(o_ref.dtype)
        return
    (acc_ref,) = acc
    k = pl.program_id(kaxis)

    @pl.when(k == 0)
    def _():
        acc_ref[...] = prod

    @pl.when(k > 0)
    def _():
        acc_ref[...] += prod

    @pl.when(k == nk - 1)
    def _():
        o_ref[...] = acc_ref[...].astype(o_ref.dtype)


def _matmul(a, b, out_dtype=F32, tm=1024, tn=512, tk=4096, name="matmul"):
    M, K = a.shape
    batched = b.ndim == 3
    N = b.shape[-1]
    assert b.shape[-2] == K
    tm = _pick_tile(M, tm, V7X_SUBLANES)
    tn = _pick_tile(N, tn)
    tk = _pick_tile(K, tk)
    nk = K // tk
    abytes = tm * tk * a.dtype.itemsize
    bbytes = tk * tn * b.dtype.itemsize
    obytes = tm * tn * jnp.dtype(out_dtype).itemsize
    vmem = _vmem_limit(2 * (abytes + bbytes + obytes) + (tm * tn * 4 if nk > 1 else 0)
                       + tm * tn * 4 + (tm * tk + tk * tn) * 2)
    scratch = [pltpu.VMEM((tm, tn), F32)] if nk > 1 else []
    if batched:
        G = b.shape[0]
        grid = (M // tm, G, N // tn, nk)
        in_specs = [pl.BlockSpec((tm, tk), lambda i, g, j, k: (i, k)),
                    pl.BlockSpec((None, tk, tn), lambda i, g, j, k: (g, k, j))]
        out_specs = pl.BlockSpec((None, tm, tn), lambda i, g, j, k: (g, i, j))
        out_shape = jax.ShapeDtypeStruct((G, M, N), out_dtype)
        sem = ("parallel", "parallel", "parallel", "arbitrary")
        kaxis = 3
    else:
        grid = (M // tm, N // tn, nk)
        in_specs = [pl.BlockSpec((tm, tk), lambda i, j, k: (i, k)),
                    pl.BlockSpec((tk, tn), lambda i, j, k: (k, j))]
        out_specs = pl.BlockSpec((tm, tn), lambda i, j, k: (i, j))
        out_shape = jax.ShapeDtypeStruct((M, N), out_dtype)
        sem = ("parallel", "parallel", "arbitrary")
        kaxis = 2
    return pl.pallas_call(
        functools.partial(_mm_body, nk=nk, kaxis=kaxis),
        out_shape=out_shape, grid=grid, in_specs=in_specs, out_specs=out_specs,
        scratch_shapes=scratch, name=name,
        compiler_params=pltpu.CompilerParams(dimension_semantics=sem, vmem_limit_bytes=vmem),
    )(a, b)


def _mod_index(b, i, nlat, nb):
    return jnp.where(i < nlat, b, nb)


def _modulate_body(x_ref, m_ref, h_ref, *, shift, scale):
    m = m_ref[...]
    h_ref[...] = (x_ref[...] * (1.0 + m[scale:scale + 1]) + m[shift:shift + 1]).astype(h_ref.dtype)


def _modulate(x, mods, L, shift, scale, out_dtype=BF16, tr=256):
    B, Lt, D = x.shape
    tr = _pick_tile(math.gcd(L, Lt - L), tr, V7X_SUBLANES)
    nlat = L // tr
    blk = tr * D * (4 + jnp.dtype(out_dtype).itemsize)
    return pl.pallas_call(
        functools.partial(_modulate_body, shift=shift, scale=scale),
        out_shape=jax.ShapeDtypeStruct((B, Lt, D), out_dtype),
        grid=(B, Lt // tr),
        in_specs=[pl.BlockSpec((None, tr, D), lambda b, i: (b, i, 0)),
                  pl.BlockSpec((None, N_MOD, D), lambda b, i: (_mod_index(b, i, nlat, B), 0, 0))],
        out_specs=pl.BlockSpec((None, tr, D), lambda b, i: (b, i, 0)),
        name="modulate",
        compiler_params=pltpu.CompilerParams(dimension_semantics=("parallel", "parallel"),
                                             vmem_limit_bytes=_vmem_limit(2 * blk)),
    )(x, mods)


def _resid_ln_body(x_ref, y_ref, m_ref, m2_ref, g_ref, b_ref, xo_ref, *h_ref, alpha, gate, shift, scale):
    m = m_ref[...]
    v = alpha * x_ref[...] + m[gate:gate + 1] * y_ref[...]
    mu = jnp.mean(v, axis=-1, keepdims=True)
    vc = v - mu
    var = jnp.mean(vc * vc, axis=-1, keepdims=True)
    out = vc * lax.rsqrt(var + LN_EPS) * g_ref[...] + b_ref[...]
    xo_ref[...] = out
    if h_ref:
        m2 = m2_ref[...]
        h_ref[0][...] = (out * (1.0 + m2[scale:scale + 1]) + m2[shift:shift + 1]).astype(h_ref[0].dtype)


def _resid_ln(x, y, mods, mods2, g, b, L, alpha, gate, shift=None, scale=None, h_dtype=F32, rows=None, tr=128):
    B, Lt, D = x.shape
    rows = Lt if rows is None else rows
    tr = _pick_tile(math.gcd(L, Lt - L), tr, V7X_SUBLANES)
    nlat = L // tr
    emit_h = shift is not None
    row = pl.BlockSpec((None, tr, D), lambda b_, i: (b_, i, 0))
    modspec = pl.BlockSpec((None, N_MOD, D), lambda b_, i: (_mod_index(b_, i, nlat, B), 0, 0))
    vec = pl.BlockSpec((1, D), lambda b_, i: (0, 0))
    out_shape = [jax.ShapeDtypeStruct((B, rows, D), F32)]
    out_specs = [row]
    if emit_h:
        out_shape.append(jax.ShapeDtypeStruct((B, rows, D), h_dtype))
        out_specs.append(row)
    blk = tr * D * 4 * 4
    res = pl.pallas_call(
        functools.partial(_resid_ln_body, alpha=alpha, gate=gate, shift=shift, scale=scale),
        out_shape=out_shape, grid=(B, rows // tr),
        in_specs=[row, row, modspec, modspec, vec, vec], out_specs=out_specs,
        name="resid_ln",
        compiler_params=pltpu.CompilerParams(dimension_semantics=("parallel", "parallel"),
                                             vmem_limit_bytes=_vmem_limit(2 * blk)),
    )(x, y, mods, mods2, g.reshape(1, D), b.reshape(1, D))
    return (res[0], res[1]) if emit_h else (res[0], None)


def _qk_prep_body(p_ref, cos_ref, sin_ref, gain_ref, o_ref, *, plan):
    cos = cos_ref[...]
    sin = sin_ref[...]
    tr = cos.shape[0]
    lane = lax.broadcasted_iota(jnp.int32, (tr, HEAD_DIM), 1)
    partner = lane ^ 1
    from_prev = pltpu.roll(lane, 1, axis=1) == partner
    for blk, (norm_row, rope, post) in enumerate(plan):
        sl = slice(blk * HEAD_DIM, (blk + 1) * HEAD_DIM)
        v = p_ref[:, sl]
        if norm_row is not None:
            ms = jnp.mean(v * v, axis=-1, keepdims=True)
            v = v * lax.rsqrt(ms + EPS) * gain_ref[norm_row:norm_row + 1, :]
        if rope:
            swapped = jnp.where(from_prev, pltpu.roll(v, 1, axis=1), pltpu.roll(v, HEAD_DIM - 1, axis=1))
            v = v * cos + swapped * sin
        if post != 1.0:
            v = v * post
        o_ref[:, sl] = v.astype(o_ref.dtype)


def _qk_prep(p, cos, sin, gains, plan, tr=256):
    B, Lt, _ = p.shape
    W = len(plan) * HEAD_DIM
    tr = _pick_tile(Lt, tr, 16)
    blk = tr * W * (4 + 2) + 2 * tr * HEAD_DIM * 4
    return pl.pallas_call(
        functools.partial(_qk_prep_body, plan=tuple(plan)),
        out_shape=jax.ShapeDtypeStruct((B, Lt, W), BF16),
        grid=(B, Lt // tr),
        in_specs=[pl.BlockSpec((None, tr, W), lambda b, i: (b, i, 0)),
                  pl.BlockSpec((tr, HEAD_DIM), lambda b, i: (i, 0)),
                  pl.BlockSpec((tr, HEAD_DIM), lambda b, i: (i, 0)),
                  pl.BlockSpec(gains.shape, lambda b, i: (0, 0))],
        out_specs=pl.BlockSpec((None, tr, W), lambda b, i: (b, i, 0)),
        name="qk_prep",
        compiler_params=pltpu.CompilerParams(dimension_semantics=("parallel", "parallel"),
                                             vmem_limit_bytes=_vmem_limit(2 * blk)),
    )(p, cos, sin, gains)


def _softmax_pv(q, k_ref, v_ref, lo, kcols):
    hi = k_ref.shape[0]
    nchunk = ATTN_KEY_CHUNKS if (hi - lo) >= ATTN_MIN_CHUNKED_KEYS and (hi - lo) % (16 * ATTN_KEY_CHUNKS) == 0 else 1
    n = (hi - lo) // nchunk
    parts = []
    for c in range(nchunk):
        rows = slice(lo + c * n, lo + (c + 1) * n)
        s = lax.dot_general(q, k_ref[rows, kcols], (((1,), (1,)), ((), ())), preferred_element_type=F32)
        m = jnp.max(s, axis=-1, keepdims=True)
        e = jnp.exp(s - m)
        l = jnp.sum(e, axis=-1, keepdims=True)
        parts.append((m, l, jnp.dot(e.astype(BF16), v_ref[rows, :], preferred_element_type=F32)))
    m, l, o = parts[0]
    for m2, l2, o2 in parts[1:]:
        mn = jnp.maximum(m, m2)
        a1, a2 = jnp.exp(m - mn), jnp.exp(m2 - mn)
        m, l, o = mn, l * a1 + l2 * a2, o * a1 + o2 * a2
    return o / l


def _key_ranges(fn, nlat, L):
    tile = pl.program_id(2)

    @pl.when(tile < nlat)
    def _():
        fn(0)

    @pl.when(tile >= nlat)
    def _():
        fn(L)


def _da_attn_body(lam_ref, q_ref, k_ref, v_ref, g_ref, o_ref, *, nlat, L):
    def run(lo):
        o0 = _softmax_pv(q_ref[:, :HEAD_DIM], k_ref, v_ref, lo, slice(0, HEAD_DIM))
        o1 = _softmax_pv(q_ref[:, HEAD_DIM:], k_ref, v_ref, lo, slice(HEAD_DIM, 2 * HEAD_DIM))
        d = o0 - lam_ref[0, 0] * o1
        ms = jnp.mean(d * d, axis=-1, keepdims=True)
        o_ref[...] = (d * lax.rsqrt(ms + EPS) * g_ref[...]).astype(o_ref.dtype)

    _key_ranges(run, nlat, L)


def _gq_attn_body(q_ref, k_ref, v_ref, o_ref, *, hp, nlat, L):
    def run(lo):
        for j in range(hp):
            sl = slice(j * HEAD_DIM, (j + 1) * HEAD_DIM)
            o_ref[:, sl] = _softmax_pv(q_ref[:, sl], k_ref, v_ref, lo, slice(0, HEAD_DIM)).astype(o_ref.dtype)

    _key_ranges(run, nlat, L)


def _attention(body, a, L, groups, qw, kw, vw, ow, q_col, k_col, v_col, kv_div, scalars=(), vectors=(),
               tq=256, name="attention"):
    B, Lt, _ = a.shape
    tq = _pick_tile(math.gcd(L, Lt - L), tq, 16)
    in_specs = [pl.BlockSpec(memory_space=pltpu.SMEM) for _ in scalars]
    in_specs += [pl.BlockSpec((None, tq, qw), lambda b, h, i: (b, i, q_col + h)),
                 pl.BlockSpec((None, Lt, kw), lambda b, h, i: (b, 0, k_col + h // kv_div)),
                 pl.BlockSpec((None, Lt, vw), lambda b, h, i: (b, 0, v_col + h // kv_div))]
    in_specs += [pl.BlockSpec(vec.shape, lambda b, h, i: (0, 0)) for vec in vectors]
    nprob = max(qw // HEAD_DIM, 1)
    blk = tq * qw * 2 + Lt * (kw + vw) * 2 + tq * ow * 2
    inter = nprob * tq * Lt * (4 + 4 + 2)
    return pl.pallas_call(
        functools.partial(body, nlat=L // tq, L=L),
        out_shape=jax.ShapeDtypeStruct((B, Lt, groups * ow), BF16),
        grid=(B, groups, Lt // tq),
        in_specs=in_specs,
        out_specs=pl.BlockSpec((None, tq, ow), lambda b, h, i: (b, i, h)),
        name=name,
        compiler_params=pltpu.CompilerParams(dimension_semantics=("parallel", "parallel", "parallel"),
                                             vmem_limit_bytes=_vmem_limit(2 * blk + inter)),
    )(*scalars, a, a, a, *vectors)


def _mm_multi_body(*refs, ks):
    n = len(ks)
    b_ref, o_ref = refs[n], refs[n + 1]
    acc, off = None, 0
    for a_ref, kk in zip(refs[:n], ks):
        part = jnp.dot(a_ref[...].astype(BF16), b_ref[off:off + kk, :], preferred_element_type=F32)
        acc = part if acc is None else acc + part
        off += kk
    o_ref[...] = acc.astype(o_ref.dtype)


def _matmul_multi(a_list, b, out_dtype=F32, tm=1024, tn=512, name="matmul_multi"):
    M = a_list[0].shape[0]
    ks = tuple(a.shape[1] for a in a_list)
    K, N = b.shape
    assert sum(ks) == K
    tm = _pick_tile(M, tm, V7X_SUBLANES)
    tn = _pick_tile(N, tn)
    blk = sum(tm * kk * a.dtype.itemsize for a, kk in zip(a_list, ks)) + K * tn * b.dtype.itemsize + tm * tn * 4
    return pl.pallas_call(
        functools.partial(_mm_multi_body, ks=ks),
        out_shape=jax.ShapeDtypeStruct((M, N), out_dtype),
        grid=(M // tm, N // tn),
        in_specs=[pl.BlockSpec((tm, kk), lambda i, j: (i, 0)) for kk in ks]
        + [pl.BlockSpec((K, tn), lambda i, j: (0, j))],
        out_specs=pl.BlockSpec((tm, tn), lambda i, j: (i, j)),
        name=name,
        compiler_params=pltpu.CompilerParams(dimension_semantics=("parallel", "parallel"),
                                             vmem_limit_bytes=_vmem_limit(2 * blk + 2 * tm * tn * 4)),
    )(*a_list, b)


def _ml_post_body(hf_ref, hb_ref, o_ref, g_ref, out_ref):
    hs = hf_ref[...] + hb_ref[...]
    ms = jnp.mean(hs * hs, axis=-1, keepdims=True)
    hn = hs * lax.rsqrt(ms + EPS) * g_ref[...]
    out_ref[...] = (jax.nn.sigmoid(o_ref[...]) * hn).astype(out_ref.dtype)


def _ml_post(hf, hb, p, o_col, gains, tr=256):
    B, Lt, W = hf.shape
    H, _, d = gains.shape
    tr = _pick_tile(Lt, tr, 16)
    head = pl.BlockSpec((None, tr, d), lambda b, i, h: (b, i, h))
    blk = tr * d * (4 * 3 + 2)
    return pl.pallas_call(
        _ml_post_body,
        out_shape=jax.ShapeDtypeStruct((B, Lt, W), BF16),
        grid=(B, Lt // tr, H),
        in_specs=[head, head, pl.BlockSpec((None, tr, d), lambda b, i, h: (b, i, o_col + h)),
                  pl.BlockSpec((None, 1, d), lambda b, i, h: (h, 0, 0))],
        out_specs=head,
        name="ml_post",
        compiler_params=pltpu.CompilerParams(dimension_semantics=("parallel", "parallel", "parallel"),
                                             vmem_limit_bytes=_vmem_limit(2 * blk)),
    )(hf, hb, p, gains)


def _mlstm_dir(q, k, v, ig, fpre, c_ref, n_ref, m_ref, reverse):
    S = q.shape[0]
    lf = jax.nn.log_sigmoid(fpre)
    r = lax.broadcasted_iota(jnp.int32, (S, S), 0)
    c = lax.broadcasted_iota(jnp.int32, (S, S), 1)
    eye = r == c
    seen = (c >= r) if reverse else (c <= r)
    seen_t = (r >= c) if reverse else (r <= c)
    zero = jnp.zeros((S, S), F32)
    lf_row = jnp.sum(jnp.where(eye, lf, zero), axis=0, keepdims=True)
    ig_row = jnp.sum(jnp.where(eye, ig, zero), axis=0, keepdims=True)
    b_col = jnp.sum(jnp.where(seen, lf_row, zero), axis=1, keepdims=True)
    b_row = jnp.sum(jnp.where(seen_t, lf, zero), axis=0, keepdims=True)
    total = jnp.sum(lf, axis=0, keepdims=True)
    m_prev = m_ref[...]
    g_row = total - b_row + ig_row
    g_col = total - b_col + ig
    m_end = jnp.maximum(total + m_prev, jnp.max(g_row, axis=1, keepdims=True))
    w_end = jnp.exp(g_col - m_end)
    decay = jnp.exp(total + m_prev - m_end)
    dlog = jnp.where(seen, b_col - b_row + ig_row, -jnp.inf)
    inter = b_col + m_prev
    m_row = jnp.maximum(inter, jnp.max(dlog, axis=1, keepdims=True))
    w_inter = jnp.exp(inter - m_row)
    qb, kb, vb = q.astype(BF16), k.astype(BF16), v.astype(BF16)
    c_old = c_ref[...]
    n_old = n_ref[...]
    qk = lax.dot_general(qb, kb, (((1,), (1,)), ((), ())), preferred_element_type=F32)
    s = qk * jnp.exp(dlog - m_row)
    cq = lax.dot_general(qb, c_old.astype(BF16), (((1,), (1,)), ((), ())), preferred_element_type=F32)
    num = w_inter * cq + jnp.dot(s.astype(BF16), vb, preferred_element_type=F32)
    den = w_inter * jnp.sum(q * n_old, axis=1, keepdims=True) + jnp.sum(s, axis=1, keepdims=True)
    h = num / jnp.maximum(jnp.abs(den), jnp.exp(-m_row))
    vw_t = (v * w_end).T.astype(BF16)
    c_ref[...] = decay * c_old + jnp.dot(vw_t, kb, preferred_element_type=F32)
    n_ref[...] = decay * n_old + jnp.sum(k * w_end, axis=0, keepdims=True)
    m_ref[...] = m_end
    return h


def _mlstm_body(qf_ref, kf_ref, vf_ref, gf_ref, qb_ref, kb_ref, vb_ref, gb_ref, hf_ref, hb_ref,
                cf, nf, mf, cb, nb, mb, *, qscale):
    @pl.when(pl.program_id(2) == 0)
    def _():
        for c_ref, n_ref, m_ref in ((cf, nf, mf), (cb, nb, mb)):
            c_ref[...] = jnp.zeros_like(c_ref)
            n_ref[...] = jnp.zeros_like(n_ref)
            m_ref[...] = jnp.full_like(m_ref, M_EMPTY)

    gf = gf_ref[...]
    gb = gb_ref[...]
    hf_ref[...] = _mlstm_dir(qf_ref[...] * qscale, kf_ref[...], vf_ref[...], gf[:, 0:1], gf[:, 1:2],
                             cf, nf, mf, False)
    hb_ref[...] = _mlstm_dir(qb_ref[...] * qscale, kb_ref[...], vb_ref[...], gb[:, 2:3], gb[:, 3:4],
                             cb, nb, mb, True)


def _mlstm(p, gates, L, q_col, k_col, v_col, d, S):
    B, Lt, _ = p.shape
    H = gates.shape[1]
    nch = Lt // S
    nlat = L // S

    def fwd(t):
        return jnp.where(t < nch - nlat, nlat + t, t - (nch - nlat))

    def bwd(t):
        return nch - 1 - t

    def seg(col, order):
        return pl.BlockSpec((None, S, d), lambda b, h, t: (b, order(t), col + h))

    def gate(order):
        return pl.BlockSpec((None, None, S, 4), lambda b, h, t: (b, h, order(t), 0))

    out_f = pl.BlockSpec((None, S, d), lambda b, h, t: (b, fwd(t), h))
    out_b = pl.BlockSpec((None, S, d), lambda b, h, t: (b, bwd(t), h))
    state = [pltpu.VMEM((d, d), F32), pltpu.VMEM((1, d), F32), pltpu.VMEM((1, 1), F32)]
    blk = 8 * S * d * 4 + 2 * S * V7X_LANES * 4
    return pl.pallas_call(
        functools.partial(_mlstm_body, qscale=d ** -0.5),
        out_shape=[jax.ShapeDtypeStruct((B, Lt, H * d), F32)] * 2,
        grid=(B, H, nch),
        in_specs=[seg(q_col, fwd), seg(k_col, fwd), seg(v_col, fwd), gate(fwd),
                  seg(q_col, bwd), seg(k_col, bwd), seg(v_col, bwd), gate(bwd)],
        out_specs=[out_f, out_b],
        scratch_shapes=state + state,
        name="mlstm",
        compiler_params=pltpu.CompilerParams(dimension_semantics=("parallel", "parallel", "arbitrary"),
                                             vmem_limit_bytes=_vmem_limit(2 * blk + 16 * S * S * 4 + 4 * d * d * 4)),
    )(p, p, p, gates, p, p, p, gates)


def _row_in(src_hbm, dst_ref, sem, b, src_row, dst_row):
    return pltpu.make_async_copy(src_hbm.at[b, pl.ds(src_row, 1), :], dst_ref.at[pl.ds(dst_row, 1), :], sem)


def _row_out(src_ref, dst_hbm, sem, b, src_row, dst_row):
    return pltpu.make_async_copy(src_ref.at[pl.ds(src_row, 1), :], dst_hbm.at[b, pl.ds(dst_row, 1), :], sem)


def _expert_ffn_body(idx_ref, gate_ref, h_hbm, acc_in, w1_ref, w3_ref, w2_ref, acc_hbm, xs_ref, ys_ref,
                     sem_x, sem_y, sem_s, *, rows, tn, unroll):
    del acc_in
    b = pl.program_id(1)

    def gather(r, carry):
        row = idx_ref[0, 0, r]
        _row_in(h_hbm, xs_ref, sem_x, b, row, r).start()
        _row_in(acc_hbm, ys_ref, sem_y, b, row, r).start()
        return carry

    lax.fori_loop(0, rows, gather, 0, unroll=unroll)

    def wait_x(r, carry):
        _row_in(h_hbm, xs_ref, sem_x, b, 0, r).wait()
        return carry

    lax.fori_loop(0, rows, wait_x, 0, unroll=unroll)
    xs = xs_ref[...].astype(BF16)
    a = jnp.dot(xs, w1_ref[...], preferred_element_type=F32)
    g = jnp.dot(xs, w3_ref[...], preferred_element_type=F32)
    hid = (jax.nn.silu(a) * g).astype(BF16)

    def wait_y(r, carry):
        _row_in(acc_hbm, ys_ref, sem_y, b, 0, r).wait()
        return carry

    lax.fori_loop(0, rows, wait_y, 0, unroll=unroll)
    gate = gate_ref[...]
    for j in range(ys_ref.shape[1] // tn):
        sl = slice(j * tn, (j + 1) * tn)
        ys_ref[:, sl] += jnp.dot(hid, w2_ref[:, sl], preferred_element_type=F32) * gate

    def scatter(r, carry):
        _row_out(ys_ref, acc_hbm, sem_s, b, r, idx_ref[0, 0, r]).start()
        return carry

    lax.fori_loop(0, rows, scatter, 0, unroll=unroll)

    def wait_s(r, carry):
        _row_out(ys_ref, acc_hbm, sem_s, b, r, 0).wait()
        return carry

    lax.fori_loop(0, rows, wait_s, 0, unroll=unroll)


def _expert_ffn(hm, acc, idx, gate, w1, w3, w2, tn=1024, unroll=8):
    B, _, D = hm.shape
    E, _, F = w1.shape
    R = idx.shape[-1]
    tn = _pick_tile(D, tn)
    unroll = unroll if R % unroll == 0 else 1
    blk = 3 * D * F * 2 + R * V7X_LANES * 4
    return pl.pallas_call(
        functools.partial(_expert_ffn_body, rows=R, tn=tn, unroll=unroll),
        out_shape=jax.ShapeDtypeStruct(acc.shape, F32),
        grid=(E, B),
        in_specs=[pl.BlockSpec((1, 1, R), lambda e, b: (b * E + e, 0, 0), memory_space=pltpu.SMEM),
                  pl.BlockSpec((None, R, 1), lambda e, b: (b * E + e, 0, 0)),
                  pl.BlockSpec(memory_space=pl.ANY),
                  pl.BlockSpec(memory_space=pl.ANY),
                  pl.BlockSpec((None, D, F), lambda e, b: (e, 0, 0)),
                  pl.BlockSpec((None, D, F), lambda e, b: (e, 0, 0)),
                  pl.BlockSpec((None, F, D), lambda e, b: (e, 0, 0))],
        out_specs=pl.BlockSpec(memory_space=pl.ANY),
        input_output_aliases={3: 0},
        scratch_shapes=[pltpu.VMEM((R, D), F32), pltpu.VMEM((R, D), F32),
                        pltpu.SemaphoreType.DMA(()), pltpu.SemaphoreType.DMA(()), pltpu.SemaphoreType.DMA(())],
        name="expert_ffn",
        compiler_params=pltpu.CompilerParams(dimension_semantics=("arbitrary", "arbitrary"),
                                             vmem_limit_bytes=_vmem_limit(2 * blk + R * D * 10 + R * tn * 4
                                                                          + R * F * 10)),
    )(idx, gate, hm, acc, w1, w3, w2)


def _lane_cumsum_excl(x):
    rows, T = x.shape
    r = lax.broadcasted_iota(jnp.int32, (V7X_LANES, V7X_LANES), 0)
    c = lax.broadcasted_iota(jnp.int32, (V7X_LANES, V7X_LANES), 1)
    tri = jnp.where(r <= c, 1.0, 0.0).astype(BF16)
    off = jnp.zeros((rows, 1), F32)
    outs = []
    for g in range(T // V7X_LANES):
        xg = x[:, g * V7X_LANES:(g + 1) * V7X_LANES]
        inc = jnp.dot(xg.astype(BF16), tri, preferred_element_type=F32)
        outs.append(inc - xg + off)
        off = off + inc[:, V7X_LANES - 1:V7X_LANES]
    return jnp.concatenate(outs, axis=1)


def _route_body(aff_ref, table_ref, out_ref, pos_ref, *, cap, chunk):
    aff = aff_ref[...]
    E, T = aff.shape
    bits = lax.bitcast_convert_type(aff, jnp.int32)
    kth = jnp.zeros((E, 1), jnp.int32)
    for bit in range(30, -1, -1):
        cand = kth | (1 << bit)
        cnt = jnp.sum(jnp.where(bits >= cand, 1.0, 0.0), axis=1, keepdims=True)
        kth = jnp.where(cnt >= cap, cand, kth)
    above = jnp.where(bits > kth, 1.0, 0.0)
    tied = jnp.where(bits == kth, 1.0, 0.0)
    need = cap - jnp.sum(above, axis=1, keepdims=True)
    sel = above + tied * jnp.where(_lane_cumsum_excl(tied) < need, 1.0, 0.0)
    pos_ref[...] = jnp.where(sel > 0.0, _lane_cumsum_excl(sel), -1.0)
    slot = lax.broadcasted_iota(jnp.int32, (chunk, T), 0).astype(F32)

    def compact(e, carry):
        pos_e = pos_ref[pl.ds(e, 1), :]
        for r0 in range(0, cap, chunk):
            onehot = jnp.where(slot + float(r0) == pos_e, 1.0, 0.0).astype(BF16)
            out_ref[e, r0:r0 + chunk, :] = jnp.dot(onehot, table_ref[...], preferred_element_type=F32)
        return carry

    lax.fori_loop(0, E, compact, 0)


def _route(aff_t, table, cap):
    B, E, T = aff_t.shape
    chunk = _pick_tile(cap, V7X_LANES, 16)
    blk = E * T * 4 + T * V7X_LANES * 2 + E * cap * V7X_LANES * 4
    return pl.pallas_call(
        functools.partial(_route_body, cap=cap, chunk=chunk),
        out_shape=jax.ShapeDtypeStruct((B, E, cap, V7X_LANES), F32),
        grid=(B,),
        in_specs=[pl.BlockSpec((None, E, T), lambda b: (b, 0, 0)),
                  pl.BlockSpec((None, T, V7X_LANES), lambda b: (b, 0, 0))],
        out_specs=pl.BlockSpec((None, E, cap, V7X_LANES), lambda b: (b, 0, 0, 0)),
        scratch_shapes=[pltpu.VMEM((E, T), F32)],
        name="route",
        compiler_params=pltpu.CompilerParams(dimension_semantics=("parallel",),
                                             vmem_limit_bytes=_vmem_limit(2 * blk + 8 * chunk * T * 4 + 8 * E * T * 4)),
    )(aff_t, table)


def _route_tables(aff):
    B, T, E = aff.shape
    a1 = aff.astype(BF16)
    r1 = aff - a1.astype(F32)
    a2 = r1.astype(BF16)
    a3 = (r1 - a2.astype(F32)).astype(BF16)
    t = jnp.arange(T, dtype=jnp.int32)
    digits = jnp.stack([t // ROUTE_DIGIT, t % ROUTE_DIGIT], axis=-1).astype(BF16)
    digits = jnp.broadcast_to(digits[None], (B, T, 2))
    return _pad_to(jnp.concatenate([a1, a2, a3, digits], axis=-1), 2, V7X_LANES)


def _route_unpack(packed, E, row0):
    parts = packed[..., :3 * E].reshape(packed.shape[:3] + (3, E))
    own = jnp.eye(E, dtype=F32)[None, :, None, None, :]
    gate = jnp.sum(parts * own, axis=(-1, -2))
    idx = jnp.round(packed[..., 3 * E] * ROUTE_DIGIT + packed[..., 3 * E + 1]).astype(jnp.int32) + row0
    return idx, gate


def _short_conv_body(x_ref, w_ref, b_ref, o_ref):
    x = x_ref[...]
    n = x.shape[0]
    row = lax.broadcasted_iota(jnp.int32, x.shape, 0)
    r1 = pltpu.roll(x, 1, axis=0)
    r2 = pltpu.roll(x, n - 1, axis=0)
    one_brings_prev = pltpu.roll(row, 1, axis=0) == jnp.where(row == 0, n - 1, row - 1)
    prev = jnp.where(row == 0, 0.0, jnp.where(one_brings_prev, r1, r2))
    nxt = jnp.where(row == n - 1, 0.0, jnp.where(one_brings_prev, r2, r1))
    w = w_ref[...]
    o_ref[...] = w[0:1] * prev + w[1:2] * x + w[2:3] * nxt + b_ref[...]


def _short_conv(p, col0, width, rows, w, b, tc=128):
    B = p.shape[0]
    r0, n = rows
    tc = _pick_tile(width, tc)
    assert r0 % n == 0 and col0 % tc == 0
    rb, cb = r0 // n, col0 // tc
    return pl.pallas_call(
        _short_conv_body,
        out_shape=jax.ShapeDtypeStruct((B, n, width), F32),
        grid=(B, width // tc),
        in_specs=[pl.BlockSpec((None, n, tc), lambda b_, j: (b_, rb, cb + j)),
                  pl.BlockSpec((w.shape[0], tc), lambda b_, j: (0, j)),
                  pl.BlockSpec((1, tc), lambda b_, j: (0, j))],
        out_specs=pl.BlockSpec((None, n, tc), lambda b_, j: (b_, 0, j)),
        name="short_conv",
        compiler_params=pltpu.CompilerParams(dimension_semantics=("parallel", "parallel"),
                                             vmem_limit_bytes=_vmem_limit(4 * n * tc * 4 + 6 * n * tc * 4)),
    )(p, w, b.reshape(1, width))


def _rope_tables(L, Lc):
    rows = L // GRID_W
    row = jnp.repeat(jnp.arange(rows, dtype=F32), GRID_W)
    col = jnp.tile(jnp.arange(GRID_W, dtype=F32), rows)
    half = HEAD_DIM // 2
    inv = ROPE_THETA ** (-jnp.arange(0, half, 2, dtype=F32) / half)
    ang = jnp.concatenate([row[:, None] * inv, col[:, None] * inv], axis=-1)
    cos, sin = jnp.cos(ang), jnp.sin(ang)
    cos2 = jnp.repeat(cos, 2, axis=-1)
    sin2 = jnp.stack([-sin, sin], axis=-1).reshape(L, HEAD_DIM)
    cos2 = jnp.concatenate([cos2, jnp.ones((Lc, HEAD_DIM), F32)], axis=0)
    sin2 = jnp.concatenate([sin2, jnp.zeros((Lc, HEAD_DIM), F32)], axis=0)
    return cos2, sin2


def _dft_matrices(L):
    N = 2 * L
    k = jnp.arange(L, dtype=jnp.int32)
    n = jnp.arange(L, dtype=jnp.int32)
    if L % DFT_FINE == 0:
        n1 = jnp.arange(L // DFT_FINE, dtype=jnp.int32) * DFT_FINE
        n2 = jnp.arange(DFT_FINE, dtype=jnp.int32)
        a = (2.0 * math.pi / N) * ((k[:, None] * n1[None, :]) % N).astype(F32)
        b = (2.0 * math.pi / N) * ((k[:, None] * n2[None, :]) % N).astype(F32)
        ca, sa, cb, sb = jnp.cos(a)[:, :, None], jnp.sin(a)[:, :, None], jnp.cos(b)[:, None, :], jnp.sin(b)[:, None, :]
        cos = (ca * cb - sa * sb).reshape(L, L)
        sin = (sa * cb + ca * sb).reshape(L, L)
    else:
        ang = (2.0 * math.pi / N) * ((k[:, None] * n[None, :]) % N).astype(F32)
        cos, sin = jnp.cos(ang), jnp.sin(ang)
    nyq = jnp.where(n % 2 == 0, 1.0, -1.0).astype(F32)[None, :]
    first = (k == 0)[:, None]
    fwd = jnp.concatenate([cos, jnp.where(first, nyq, -sin)], axis=0)
    inv_re = jnp.where(first, 1.0, 2.0) * cos / N
    inv_im = jnp.where(first, nyq / N, -2.0 * sin / N)
    inv = jnp.concatenate([inv_re, inv_im], axis=0).T
    return fwd.astype(BF16), inv.astype(BF16)


def _pad_to(a, axis, size):
    pad = [(0, 0)] * a.ndim
    pad[axis] = (0, size - a.shape[axis])
    return jnp.pad(a, pad)


def _hyena_filters(L, w1, b1, w2, b2, w3, sin_freq, width):
    bands = (HY_EMB - 1) // 2
    t = jnp.linspace(0.0, 1.0, L, dtype=F32)[:, None]
    w = (2.0 * math.pi / L) * jnp.arange(L, dtype=F32)[:, None]
    f = jnp.linspace(1e-4, bands - 1, bands, dtype=F32)[None, :]
    z = jnp.concatenate([t, jnp.cos(f * w), -jnp.sin(f * w)], axis=-1)
    ffn = w1.shape[1]
    a = _matmul(_pad_to(z, 1, V7X_LANES), _pad_to(_pad_to(w1, 0, V7X_LANES), 1, V7X_LANES), name="hy_f1")
    a = jnp.sin(sin_freq[0] * (a[:, :ffn] + b1))
    a = _matmul(_pad_to(a, 1, V7X_LANES), _pad_to(_pad_to(w2, 0, V7X_LANES), 1, V7X_LANES), name="hy_f2")
    a = jnp.sin(sin_freq[1] * (a[:, :ffn] + b2))
    h = _matmul(_pad_to(a, 1, V7X_LANES), _pad_to(w3, 0, V7X_LANES), name="hy_f3")
    h = h.reshape(L, HY_ORDER, 2, width)
    rates = jnp.linspace(HY_SLOW_RATE, HY_FAST_RATE, width, dtype=F32)
    h = h * jnp.exp(-t[:, :, None, None] * rates)
    return h * lax.rsqrt(jnp.sum(jnp.square(h), axis=(0, 2), keepdims=True) + EPS)


def _hyena(u, filt, bias, dft):
    B, L, C3 = u.shape
    C = C3 // 3
    fwd, inv = dft
    first = (jnp.arange(L) == 0)[:, None]
    hf = filt[:, :, 0]
    hb = jnp.where(first[:, :, None], 0.0, filt[:, :, 1])
    hsum = jnp.moveaxis(hf + hb, 1, 0)
    hdif = jnp.moveaxis(hf - hb, 1, 0)
    sign = jnp.where(jnp.arange(L) % 2 == 0, 1.0, -1.0).astype(F32)[None, :, None]
    Kr_all = _matmul(fwd[:L], hsum.astype(BF16), name="hy_dft_filter")
    Ki_all = _matmul(fwd[L:], hdif.astype(BF16), name="hy_dft_filter")
    nyq = jnp.sum(hsum * sign, axis=1, keepdims=True)
    Ki_all = jnp.where(first, nyq, Ki_all)
    zb = u[..., :C].astype(BF16)
    z = u
    for o in range(HY_ORDER):
        yr, yi = _dft_fwd_mul(fwd, zb, Kr_all[o], Ki_all[o])
        last = o == HY_ORDER - 1
        res = _dft_inv_gate(inv, yr, yi, u, (1 + o) * C, z, 0, bias[o].reshape(1, C), emit_f32=not last)
        if last:
            return res[0]
        z, zb = res


def _dft_fwd_mul_body(fr_ref, fi_ref, z_ref, kr_ref, ki_ref, yr_ref, yi_ref):
    z = z_ref[...]
    ur = jnp.dot(fr_ref[...], z, preferred_element_type=F32)
    ui = jnp.dot(fi_ref[...], z, preferred_element_type=F32)
    kr, ki = kr_ref[...], ki_ref[...]
    tm = ur.shape[0]
    row = lax.broadcasted_iota(jnp.int32, (tm, 1), 0) + pl.program_id(0) * tm
    first = row == 0
    ii = ui * ki
    yr_ref[...] = (ur * kr - jnp.where(first, 0.0, ii)).astype(yr_ref.dtype)
    yi_ref[...] = jnp.where(first, ii, ur * ki + ui * kr).astype(yi_ref.dtype)


def _dft_fwd_mul(fwd, zb, kr, ki, tm=512, tn=512):
    B, L, C = zb.shape
    tm = _pick_tile(L, tm)
    tn = _pick_tile(C, tn)
    nlo = L // tm
    blk = 2 * tm * L * 2 + L * tn * 2 + 2 * tm * tn * 4 + 2 * tm * tn * 2
    out = jax.ShapeDtypeStruct((B, L, C), BF16)
    return pl.pallas_call(
        _dft_fwd_mul_body,
        out_shape=[out, out],
        grid=(nlo, B, C // tn),
        in_specs=[pl.BlockSpec((tm, L), lambda i, b, j: (i, 0)),
                  pl.BlockSpec((tm, L), lambda i, b, j: (nlo + i, 0)),
                  pl.BlockSpec((None, L, tn), lambda i, b, j: (b, 0, j)),
                  pl.BlockSpec((tm, tn), lambda i, b, j: (i, j)),
                  pl.BlockSpec((tm, tn), lambda i, b, j: (i, j))],
        out_specs=[pl.BlockSpec((None, tm, tn), lambda i, b, j: (b, i, j))] * 2,
        name="hy_dft_fwd",
        compiler_params=pltpu.CompilerParams(dimension_semantics=("parallel", "parallel", "parallel"),
                                             vmem_limit_bytes=_vmem_limit(2 * blk + 4 * tm * tn * 4)),
    )(fwd, fwd, zb, kr, ki)


def _dft_inv_gate_body(ir_ref, ii_ref, yr_ref, yi_ref, g_ref, z_ref, b_ref, *outs, emit_f32):
    conv = (jnp.dot(ir_ref[...], yr_ref[...], preferred_element_type=F32)
            + jnp.dot(ii_ref[...], yi_ref[...], preferred_element_type=F32))
    zn = g_ref[...] * (conv + b_ref[...] * z_ref[...])
    if emit_f32:
        outs[0][...] = zn
    outs[-1][...] = zn.astype(BF16)


def _dft_inv_gate(inv, yr, yi, u, gate_col, z, z_col, bias, emit_f32, tm=512, tn=512):
    B, L, C = yr.shape
    tm = _pick_tile(L, tm)
    tn = _pick_tile(C, tn)
    gj, zj = gate_col // tn, z_col // tn
    tile = pl.BlockSpec((None, tm, tn), lambda i, b, j: (b, i, j))
    spec = pl.BlockSpec((None, L, tn), lambda i, b, j: (b, 0, j))
    blk = 2 * tm * L * 2 + 2 * L * tn * 2 + 3 * tm * tn * 4 + tm * tn * 2
    out_shape = [jax.ShapeDtypeStruct((B, L, C), BF16)]
    if emit_f32:
        out_shape.insert(0, jax.ShapeDtypeStruct((B, L, C), F32))
    return pl.pallas_call(
        functools.partial(_dft_inv_gate_body, emit_f32=emit_f32),
        out_shape=out_shape,
        grid=(L // tm, B, C // tn),
        in_specs=[pl.BlockSpec((tm, L), lambda i, b, j: (i, 0)),
                  pl.BlockSpec((tm, L), lambda i, b, j: (i, 1)),
                  spec, spec,
                  pl.BlockSpec((None, tm, tn), lambda i, b, j: (b, i, gj + j)),
                  pl.BlockSpec((None, tm, tn), lambda i, b, j: (b, i, zj + j)),
                  pl.BlockSpec((1, tn), lambda i, b, j: (0, j))],
        out_specs=[tile] * len(out_shape),
        name="hy_dft_inv",
        compiler_params=pltpu.CompilerParams(dimension_semantics=("parallel", "parallel", "parallel"),
                                             vmem_limit_bytes=_vmem_limit(2 * blk + 2 * tm * tn * 4)),
    )(inv, inv, yr, yi, u, z, bias)


def kernel(x, c, ctx, c_ctx, ada_down, ada_up, ada_b, w_in, w_out, da_lambda, da_subln, gq_qk_norm,
           hy_conv_w, hy_conv_b, hy_f_w1, hy_f_b1, hy_f_w2, hy_f_b2, hy_f_w3, hy_sin_freq, hy_bias,
           ml_gate_b, ml_norm, ln_g, ln_b, router, ex_w1, ex_w3, ex_w2):
    B, L, D = x.shape
    Lc = ctx.shape[1]
    Lt = L + Lc
    depth = w_in.shape[0]
    E = router.shape[-1]
    alpha = (2 * depth) ** 0.25
    da_heads = D // (8 * HEAD_DIM)
    da_v = 2 * HEAD_DIM
    gq_heads = D // (4 * HEAD_DIM)
    gq_kv = gq_heads // 4
    gq_rep = gq_heads // gq_kv
    hy_w = D // 4
    ml_w = D - da_heads * da_v - gq_heads * HEAD_DIM - hy_w
    ml_d = ml_w // ML_HEADS
    widths = [da_heads * 2 * HEAD_DIM, da_heads * 2 * HEAD_DIM, da_heads * da_v, gq_heads * HEAD_DIM,
              gq_kv * HEAD_DIM, gq_kv * HEAD_DIM, hy_w, hy_w, hy_w, ml_w, ml_w, ml_w, ml_w, 4 * ML_HEADS]
    offs = np.concatenate([[0], np.cumsum(widths)]).astype(int)
    o_daq, o_dak, o_dav, o_gqq, o_gqk, o_gqv, o_hy, _, _, o_mlq, o_mlk, o_mlv, o_mlo, o_mlg, n_in = offs
    n_pad = -(-n_in // 512) * 512
    att_w = o_hy
    S = min(256, Lc)
    assert L % S == 0 and Lc % S == 0 and L % Lc == 0

    qs = HEAD_DIM ** -0.5
    n_da = 2 * da_heads
    plan = ([(None, True, qs)] * n_da + [(None, True, 1.0)] * n_da + [(None, False, 1.0)] * n_da
            + [(0, True, qs)] * gq_heads + [(1, True, 1.0)] * gq_kv + [(None, False, 1.0)] * gq_kv)
    cos2, sin2 = _rope_tables(L, Lc)
    dft_lat = _dft_matrices(L)
    dft_ctx = _dft_matrices(Lc)

    xs = jnp.concatenate([x, ctx], axis=1)
    cond = jnp.concatenate([c, c_ctx[None]], axis=0)
    cond = _pad_to(jax.nn.silu(cond), 0, V7X_SUBLANES)

    mods_all = []
    for layer in range(depth):
        m = _matmul(_matmul(cond, ada_down[layer], name="ada_down"), ada_up[layer], name="ada_up")
        mods_all.append((m[:B + 1] + ada_b[layer]).reshape(B + 1, N_MOD, D))

    for layer in range(depth):
        last = layer == depth - 1
        mods = mods_all[layer]

        if layer == 0:
            h = _modulate(xs, mods, L, 0, 1)
        w_in_l = _pad_to(w_in[layer], 1, n_pad).astype(BF16)
        p = _matmul(h.reshape(B * Lt, D), w_in_l, name="in_proj").reshape(B, Lt, n_pad)

        a = _qk_prep(p, cos2, sin2, gq_qk_norm[layer], plan)
        lam_init = 0.8 - 0.6 * math.exp(-0.3 * layer)
        lp = da_lambda[layer]
        lam = jnp.exp(jnp.sum(lp[0] * lp[1])) - jnp.exp(jnp.sum(lp[2] * lp[3])) + lam_init
        da = _attention(_da_attn_body, a, L, groups=da_heads, qw=da_v, kw=da_v, vw=da_v, ow=da_v,
                        q_col=o_daq // da_v, k_col=o_dak // da_v, v_col=o_dav // da_v, kv_div=1,
                        scalars=(lam.reshape(1, 1),),
                        vectors=((da_subln[layer] * (1.0 - lam_init)).reshape(1, da_v),), name="da_attention")
        gq_w = GQ_HEADS_PER_STEP * HEAD_DIM
        gq = _attention(functools.partial(_gq_attn_body, hp=GQ_HEADS_PER_STEP), a, L,
                        groups=gq_heads // GQ_HEADS_PER_STEP, qw=gq_w, kw=HEAD_DIM, vw=HEAD_DIM, ow=gq_w,
                        q_col=o_gqq // gq_w, k_col=o_gqk // HEAD_DIM, v_col=o_gqv // HEAD_DIM,
                        kv_div=gq_rep // GQ_HEADS_PER_STEP, name="gq_attention")

        conv_args = (hy_conv_w[layer], hy_conv_b[layer])
        filt_args = (hy_f_w1[layer], hy_f_b1[layer], hy_f_w2[layer], hy_f_b2[layer], hy_f_w3[layer],
                     hy_sin_freq[layer], hy_w)
        u_lat = _short_conv(p, o_hy, 3 * hy_w, (0, L), *conv_args)
        u_ctx = _short_conv(p, o_hy, 3 * hy_w, (L, Lc), *conv_args)
        hy_lat = _hyena(u_lat, _hyena_filters(L, *filt_args), hy_bias[layer], dft_lat)
        hy_ctx = _hyena(u_ctx, _hyena_filters(Lc, *filt_args), hy_bias[layer], dft_ctx)
        hy = jnp.concatenate([hy_lat, hy_ctx], axis=1)

        g = (p[:, :, o_mlg:o_mlg + 4 * ML_HEADS] + ml_gate_b[layer]).reshape(B, Lt, 4, ML_HEADS)
        gates = jnp.transpose(g, (0, 3, 1, 2))
        hf, hb = _mlstm(p, gates, L, o_mlq // ml_d, o_mlk // ml_d, o_mlv // ml_d, ml_d, S)
        ml = _ml_post(hf, hb, p, o_mlo // ml_d, ml_norm[layer].reshape(ML_HEADS, 1, ml_d))

        slabs = [s.reshape(B * Lt, s.shape[-1]) for s in (da, gq, hy, ml)]
        mix = _matmul_multi(slabs, w_out[layer].astype(BF16), name="out_proj").reshape(B, Lt, D)

        xs, hm = _resid_ln(xs, mix, mods, mods, ln_g[layer, 0], ln_b[layer, 0], L, alpha, 2, 3, 4)

        logits = _matmul(hm.reshape(B * Lt, D), _pad_to(router[layer], 1, V7X_LANES), name="router")
        aff = jax.nn.softmax(logits[:, :E].reshape(B, Lt, E), axis=-1)
        cap, cap_c = CAPACITY_FACTOR * L // E, CAPACITY_FACTOR * Lc // E
        idx_l, gate_l = _route_unpack(_route(jnp.swapaxes(aff[:, :L], 1, 2), _route_tables(aff[:, :L]), cap), E, 0)
        idx_c, gate_c = _route_unpack(_route(jnp.swapaxes(aff[:, L:], 1, 2), _route_tables(aff[:, L:]), cap_c), E, L)
        idx = jnp.concatenate([idx_l, idx_c], axis=-1)
        gate = jnp.concatenate([gate_l, gate_c], axis=-1)
        R = cap + cap_c
        ffn = _expert_ffn(hm, jnp.zeros((B, Lt, D), F32), idx.reshape(B * E, 1, R), gate.reshape(B * E, R, 1),
                          ex_w1[layer].astype(BF16), ex_w3[layer].astype(BF16), ex_w2[layer].astype(BF16))

        if last:
            xs, _ = _resid_ln(xs, ffn, mods, mods, ln_g[layer, 1], ln_b[layer, 1], L, alpha, 5, rows=L)
        else:
            xs, h = _resid_ln(xs, ffn, mods, mods_all[layer + 1], ln_g[layer, 1], ln_b[layer, 1], L, alpha, 5, 0, 1,
                              h_dtype=BF16)
    return xs
```

```python
import functools
import math

import jax
import jax.numpy as jnp
import numpy as np
from jax import lax
from jax.experimental import pallas as pl
from jax.experimental.pallas import tpu as pltpu

HEAD_DIM = 128
GRID_W = 64
ROPE_THETA = 10000.0
HY_ORDER = 2
HY_EMB = 33
HY_SLOW_RATE = -math.log(1e-2) / 1.5
HY_FAST_RATE = -math.log(1e-2) / 0.3
ML_HEADS = 4
M_EMPTY = -1e30
CAPACITY_FACTOR = 2
N_MOD = 6
GQ_HEADS_PER_STEP = 2
GQ_QUERY_TILE = 256
GQ_KEY_CHUNKS = 1
DA_KEY_CHUNKS = 2
ATTN_MIN_CHUNKED_KEYS = 1024
DFT_FINE = 64
ROUTE_DIGIT = 64
EPS = 1e-6
LN_EPS = 1e-5

V7X_LANES = 128
V7X_SUBLANES = 8
V7X_VMEM_BYTES = 64 * 1024 * 1024
V7X_VMEM_REQUEST_CAP = 56 * 1024 * 1024
MIN_VMEM_REQUEST = 32 * 1024 * 1024

BF16 = jnp.bfloat16
F32 = jnp.float32


def _pick_tile(n, pref, mult=V7X_LANES):
    if n <= pref:
        return n
    t = (pref // mult) * mult
    while t >= mult:
        if n % t == 0:
            return t
        t -= mult
    return n


def _vmem_limit(block_bytes):
    return int(min(max(block_bytes * 5 // 4, MIN_VMEM_REQUEST), V7X_VMEM_REQUEST_CAP))


def _mm_body(a_ref, b_ref, o_ref, *acc, nk, kaxis):
    prod = jnp.dot(a_ref[...].astype(BF16), b_ref[...].astype(BF16), preferred_element_type=F32)
    if nk == 1:
        o_ref[...] = prod.astype(o_ref.dtype)
        return
    (acc_ref,) = acc
    k = pl.program_id(kaxis)

    @pl.when(k == 0)
    def _():
        acc_ref[...] = prod

    @pl.when(k > 0)
    def _():
        acc_ref[...] += prod

    @pl.when(k == nk - 1)
    def _():
        o_ref[...] = acc_ref[...].astype(o_ref.dtype)


def _matmul(a, b, out_dtype=F32, tm=1024, tn=512, tk=4096, name="matmul"):
    M, K = a.shape
    batched = b.ndim == 3
    N = b.shape[-1]
    assert b.shape[-2] == K
    tm = _pick_tile(M, tm, V7X_SUBLANES)
    tn = _pick_tile(N, tn)
    tk = _pick_tile(K, tk)
    nk = K // tk
    abytes = tm * tk * a.dtype.itemsize
    bbytes = tk * tn * b.dtype.itemsize
    obytes = tm * tn * jnp.dtype(out_dtype).itemsize
    vmem = _vmem_limit(2 * (abytes + bbytes + obytes) + (tm * tn * 4 if nk > 1 else 0)
                       + tm * tn * 4 + (tm * tk + tk * tn) * 2)
    scratch = [pltpu.VMEM((tm, tn), F32)] if nk > 1 else []
    if batched:
        G = b.shape[0]
        grid = (M // tm, G, N // tn, nk)
        in_specs = [pl.BlockSpec((tm, tk), lambda i, g, j, k: (i, k)),
                    pl.BlockSpec((None, tk, tn), lambda i, g, j, k: (g, k, j))]
        out_specs = pl.BlockSpec((None, tm, tn), lambda i, g, j, k: (g, i, j))
        out_shape = jax.ShapeDtypeStruct((G, M, N), out_dtype)
        sem = ("parallel", "parallel", "parallel", "arbitrary")
        kaxis = 3
    else:
        grid = (M // tm, N // tn, nk)
        in_specs = [pl.BlockSpec((tm, tk), lambda i, j, k: (i, k)),
                    pl.BlockSpec((tk, tn), lambda i, j, k: (k, j))]
        out_specs = pl.BlockSpec((tm, tn), lambda i, j, k: (i, j))
        out_shape = jax.ShapeDtypeStruct((M, N), out_dtype)
        sem = ("parallel", "parallel", "arbitrary")
        kaxis = 2
    return pl.pallas_call(
        functools.partial(_mm_body, nk=nk, kaxis=kaxis),
        out_shape=out_shape, grid=grid, in_specs=in_specs, out_specs=out_specs,
        scratch_shapes=scratch, name=name,
        compiler_params=pltpu.CompilerParams(dimension_semantics=sem, vmem_limit_bytes=vmem),
    )(a, b)


def _mod_index(b, i, nlat, nb):
    return jnp.where(i < nlat, b, nb)


def _modulate_body(x_ref, m_ref, h_ref, *, shift, scale):
    m = m_ref[...]
    h_ref[...] = (x_ref[...] * (1.0 + m[scale:scale + 1]) + m[shift:shift + 1]).astype(h_ref.dtype)


def _modulate(x, mods, L, shift, scale, out_dtype=BF16, tr=256):
    B, Lt, D = x.shape
    tr = _pick_tile(math.gcd(L, Lt - L), tr, V7X_SUBLANES)
    nlat = L // tr
    blk = tr * D * (4 + jnp.dtype(out_dtype).itemsize)
    return pl.pallas_call(
        functools.partial(_modulate_body, shift=shift, scale=scale),
        out_shape=jax.ShapeDtypeStruct((B, Lt, D), out_dtype),
        grid=(B, Lt // tr),
        in_specs=[pl.BlockSpec((None, tr, D), lambda b, i: (b, i, 0)),
                  pl.BlockSpec((None, N_MOD, D), lambda b, i: (_mod_index(b, i, nlat, B), 0, 0))],
        out_specs=pl.BlockSpec((None, tr, D), lambda b, i: (b, i, 0)),
        name="modulate",
        compiler_params=pltpu.CompilerParams(dimension_semantics=("parallel", "parallel"),
                                             vmem_limit_bytes=_vmem_limit(2 * blk)),
    )(x, mods)


def _resid_ln_body(x_ref, y_ref, m_ref, m2_ref, g_ref, b_ref, xo_ref, *h_ref, alpha, gate, shift, scale):
    m = m_ref[...]
    v = alpha * x_ref[...] + m[gate:gate + 1] * y_ref[...]
    mu = jnp.mean(v, axis=-1, keepdims=True)
    vc = v - mu
    var = jnp.mean(vc * vc, axis=-1, keepdims=True)
    out = vc * lax.rsqrt(var + LN_EPS) * g_ref[...] + b_ref[...]
    xo_ref[...] = out
    if h_ref:
        m2 = m2_ref[...]
        h_ref[0][...] = (out * (1.0 + m2[scale:scale + 1]) + m2[shift:shift + 1]).astype(h_ref[0].dtype)


def _resid_ln(x, y, mods, mods2, g, b, L, alpha, gate, shift=None, scale=None, h_dtype=F32, rows=None, tr=256):
    B, Lt, D = x.shape
    rows = Lt if rows is None else rows
    tr = _pick_tile(math.gcd(L, Lt - L), tr, V7X_SUBLANES)
    nlat = L // tr
    emit_h = shift is not None
    row = pl.BlockSpec((None, tr, D), lambda b_, i: (b_, i, 0))
    modspec = pl.BlockSpec((None, N_MOD, D), lambda b_, i: (_mod_index(b_, i, nlat, B), 0, 0))
    vec = pl.BlockSpec((1, D), lambda b_, i: (0, 0))
    out_shape = [jax.ShapeDtypeStruct((B, rows, D), F32)]
    out_specs = [row]
    if emit_h:
        out_shape.append(jax.ShapeDtypeStruct((B, rows, D), h_dtype))
        out_specs.append(row)
    blk = tr * D * 4 * 4
    res = pl.pallas_call(
        functools.partial(_resid_ln_body, alpha=alpha, gate=gate, shift=shift, scale=scale),
        out_shape=out_shape, grid=(B, rows // tr),
        in_specs=[row, row, modspec, modspec, vec, vec], out_specs=out_specs,
        name="resid_ln",
        compiler_params=pltpu.CompilerParams(dimension_semantics=("parallel", "parallel"),
                                             vmem_limit_bytes=_vmem_limit(2 * blk)),
    )(x, y, mods, mods2, g.reshape(1, D), b.reshape(1, D))
    return (res[0], res[1]) if emit_h else (res[0], None)


def _qk_prep_body(p_ref, cos_ref, sin_ref, gain_ref, o_ref, *, plan):
    cos = cos_ref[...]
    sin = sin_ref[...]
    tr = cos.shape[0]
    lane = lax.broadcasted_iota(jnp.int32, (tr, HEAD_DIM), 1)
    partner = lane ^ 1
    from_prev = pltpu.roll(lane, 1, axis=1) == partner
    for blk, (norm_row, rope, post) in enumerate(plan):
        sl = slice(blk * HEAD_DIM, (blk + 1) * HEAD_DIM)
        v = p_ref[:, sl]
        if norm_row is not None:
            ms = jnp.mean(v * v, axis=-1, keepdims=True)
            v = v * lax.rsqrt(ms + EPS) * gain_ref[norm_row:norm_row + 1, :]
        if rope:
            swapped = jnp.where(from_prev, pltpu.roll(v, 1, axis=1), pltpu.roll(v, HEAD_DIM - 1, axis=1))
            v = v * cos + swapped * sin
        if post != 1.0:
            v = v * post
        o_ref[:, sl] = v.astype(o_ref.dtype)


def _qk_prep(p, cos, sin, gains, plan, tr=256):
    B, Lt, _ = p.shape
    W = len(plan) * HEAD_DIM
    tr = _pick_tile(Lt, tr, 16)
    blk = tr * W * (4 + 2) + 2 * tr * HEAD_DIM * 4
    return pl.pallas_call(
        functools.partial(_qk_prep_body, plan=tuple(plan)),
        out_shape=jax.ShapeDtypeStruct((B, Lt, W), BF16),
        grid=(B, Lt // tr),
        in_specs=[pl.BlockSpec((None, tr, W), lambda b, i: (b, i, 0)),
                  pl.BlockSpec((tr, HEAD_DIM), lambda b, i: (i, 0)),
                  pl.BlockSpec((tr, HEAD_DIM), lambda b, i: (i, 0)),
                  pl.BlockSpec(gains.shape, lambda b, i: (0, 0))],
        out_specs=pl.BlockSpec((None, tr, W), lambda b, i: (b, i, 0)),
        name="qk_prep",
        compiler_params=pltpu.CompilerParams(dimension_semantics=("parallel", "parallel"),
                                             vmem_limit_bytes=_vmem_limit(2 * blk)),
    )(p, cos, sin, gains)


def _softmax_pv(q, k_ref, v_ref, lo, kcols, nchunk):
    hi = k_ref.shape[0]
    if (hi - lo) < ATTN_MIN_CHUNKED_KEYS or (hi - lo) % (16 * nchunk) != 0:
        nchunk = 1
    n = (hi - lo) // nchunk
    parts = []
    for c in range(nchunk):
        rows = slice(lo + c * n, lo + (c + 1) * n)
        s = lax.dot_general(q, k_ref[rows, kcols], (((1,), (1,)), ((), ())), preferred_element_type=F32)
        m = jnp.max(s, axis=-1, keepdims=True)
        e = jnp.exp2(s - m)
        l = jnp.sum(e, axis=-1, keepdims=True)
        parts.append((m, l, jnp.dot(e.astype(BF16), v_ref[rows, :], preferred_element_type=F32)))
    m, l, o = parts[0]
    for m2, l2, o2 in parts[1:]:
        mn = jnp.maximum(m, m2)
        a1, a2 = jnp.exp2(m - mn), jnp.exp2(m2 - mn)
        m, l, o = mn, l * a1 + l2 * a2, o * a1 + o2 * a2
    return o / l


def _key_ranges(fn, nlat, L):
    tile = pl.program_id(2)

    @pl.when(tile < nlat)
    def _():
        fn(0)

    @pl.when(tile >= nlat)
    def _():
        fn(L)


def _da_attn_body(lam_ref, q_ref, k_ref, v_ref, g_ref, o_ref, *, nlat, L, nchunk):
    def run(lo):
        o0 = _softmax_pv(q_ref[:, :HEAD_DIM], k_ref, v_ref, lo, slice(0, HEAD_DIM), nchunk)
        o1 = _softmax_pv(q_ref[:, HEAD_DIM:], k_ref, v_ref, lo, slice(HEAD_DIM, 2 * HEAD_DIM), nchunk)
        d = o0 - lam_ref[0, 0] * o1
        ms = jnp.mean(d * d, axis=-1, keepdims=True)
        o_ref[...] = (d * lax.rsqrt(ms + EPS) * g_ref[...]).astype(o_ref.dtype)

    _key_ranges(run, nlat, L)


def _gq_attn_body(q_ref, k_ref, v_ref, o_ref, *, hp, nlat, L, nchunk):
    def run(lo):
        for j in range(hp):
            sl = slice(j * HEAD_DIM, (j + 1) * HEAD_DIM)
            o_ref[:, sl] = _softmax_pv(q_ref[:, sl], k_ref, v_ref, lo, slice(0, HEAD_DIM), nchunk).astype(o_ref.dtype)

    _key_ranges(run, nlat, L)


def _attention(body, a, L, groups, qw, kw, vw, ow, q_col, k_col, v_col, kv_div, nchunk, scalars=(), vectors=(),
               tq=256, name="attention"):
    B, Lt, _ = a.shape
    tq = _pick_tile(math.gcd(L, Lt - L), tq, 16)
    in_specs = [pl.BlockSpec(memory_space=pltpu.SMEM) for _ in scalars]
    in_specs += [pl.BlockSpec((None, tq, qw), lambda b, h, i: (b, i, q_col + h)),
                 pl.BlockSpec((None, Lt, kw), lambda b, h, i: (b, 0, k_col + h // kv_div)),
                 pl.BlockSpec((None, Lt, vw), lambda b, h, i: (b, 0, v_col + h // kv_div))]
    in_specs += [pl.BlockSpec(vec.shape, lambda b, h, i: (0, 0)) for vec in vectors]
    nprob = max(qw // HEAD_DIM, 1)
    blk = tq * qw * 2 + Lt * (kw + vw) * 2 + tq * ow * 2
    inter = nprob * tq * Lt * (4 + 4 + 2)
    return pl.pallas_call(
        functools.partial(body, nlat=L // tq, L=L, nchunk=nchunk),
        out_shape=jax.ShapeDtypeStruct((B, Lt, groups * ow), BF16),
        grid=(B, groups, Lt // tq),
        in_specs=in_specs,
        out_specs=pl.BlockSpec((None, tq, ow), lambda b, h, i: (b, i, h)),
        name=name,
        compiler_params=pltpu.CompilerParams(dimension_semantics=("parallel", "parallel", "parallel"),
                                             vmem_limit_bytes=_vmem_limit(2 * blk + inter)),
    )(*scalars, a, a, a, *vectors)


def _mm_multi_body(*refs, ks):
    n = len(ks)
    b_ref, o_ref = refs[n], refs[n + 1]
    acc, off = None, 0
    for a_ref, kk in zip(refs[:n], ks):
        part = jnp.dot(a_ref[...].astype(BF16), b_ref[off:off + kk, :], preferred_element_type=F32)
        acc = part if acc is None else acc + part
        off += kk
    o_ref[...] = acc.astype(o_ref.dtype)


def _matmul_multi(a_list, b, out_dtype=F32, tm=1024, tn=512, name="matmul_multi"):
    M = a_list[0].shape[0]
    ks = tuple(a.shape[1] for a in a_list)
    K, N = b.shape
    assert sum(ks) == K
    tm = _pick_tile(M, tm, V7X_SUBLANES)
    tn = _pick_tile(N, tn)
    blk = sum(tm * kk * a.dtype.itemsize for a, kk in zip(a_list, ks)) + K * tn * b.dtype.itemsize + tm * tn * 4
    return pl.pallas_call(
        functools.partial(_mm_multi_body, ks=ks),
        out_shape=jax.ShapeDtypeStruct((M, N), out_dtype),
        grid=(M // tm, N // tn),
        in_specs=[pl.BlockSpec((tm, kk), lambda i, j: (i, 0)) for kk in ks]
        + [pl.BlockSpec((K, tn), lambda i, j: (0, j))],
        out_specs=pl.BlockSpec((tm, tn), lambda i, j: (i, j)),
        name=name,
        compiler_params=pltpu.CompilerParams(dimension_semantics=("parallel", "parallel"),
                                             vmem_limit_bytes=_vmem_limit(2 * blk + 2 * tm * tn * 4)),
    )(*a_list, b)


def _ml_post_body(hf_ref, hb_ref, *rest, heads):
    o_refs, g_ref, out_ref = rest[:heads], rest[heads], rest[heads + 1]
    d = g_ref.shape[-1]
    for h in range(heads):
        sl = slice(h * d, (h + 1) * d)
        hs = hf_ref[:, sl] + hb_ref[:, sl]
        ms = jnp.mean(hs * hs, axis=-1, keepdims=True)
        hn = hs * lax.rsqrt(ms + EPS) * g_ref[h]
        out_ref[:, sl] = (jax.nn.sigmoid(o_refs[h][...]) * hn).astype(out_ref.dtype)


def _ml_post(hf, hb, p, o_col, gains, tr=256):
    B, Lt, W = hf.shape
    H, _, d = gains.shape
    tr = _pick_tile(Lt, tr, 16)
    full = pl.BlockSpec((None, tr, W), lambda b, i: (b, i, 0))
    blk = tr * W * (4 * 3 + 2)
    return pl.pallas_call(
        functools.partial(_ml_post_body, heads=H),
        out_shape=jax.ShapeDtypeStruct((B, Lt, W), BF16),
        grid=(B, Lt // tr),
        in_specs=[full, full]
        + [pl.BlockSpec((None, tr, d), lambda b, i, h=h: (b, i, o_col + h)) for h in range(H)]
        + [pl.BlockSpec((H, 1, d), lambda b, i: (0, 0, 0))],
        out_specs=full,
        name="ml_post",
        compiler_params=pltpu.CompilerParams(dimension_semantics=("parallel", "parallel"),
                                             vmem_limit_bytes=_vmem_limit(2 * blk)),
    )(hf, hb, *([p] * H), gains)


def _mlstm_dir(q, k, v, ig, fpre, c_ref, n_ref, m_ref, reverse):
    S = q.shape[0]
    lf = jax.nn.log_sigmoid(fpre)
    r = lax.broadcasted_iota(jnp.int32, (S, S), 0)
    c = lax.broadcasted_iota(jnp.int32, (S, S), 1)
    eye = r == c
    seen = (c >= r) if reverse else (c <= r)
    seen_t = (r >= c) if reverse else (r <= c)
    zero = jnp.zeros((S, S), F32)
    lf_row = jnp.sum(jnp.where(eye, lf, zero), axis=0, keepdims=True)
    ig_row = jnp.sum(jnp.where(eye, ig, zero), axis=0, keepdims=True)
    b_col = jnp.sum(jnp.where(seen, lf_row, zero), axis=1, keepdims=True)
    b_row = jnp.sum(jnp.where(seen_t, lf, zero), axis=0, keepdims=True)
    total = jnp.sum(lf, axis=0, keepdims=True)
    m_prev = m_ref[...]
    g_row = total - b_row + ig_row
    g_col = total - b_col + ig
    m_end = jnp.maximum(total + m_prev, jnp.max(g_row, axis=1, keepdims=True))
    w_end = jnp.exp(g_col - m_end)
    decay = jnp.exp(total + m_prev - m_end)
    dlog = jnp.where(seen, b_col - b_row + ig_row, -jnp.inf)
    inter = b_col + m_prev
    m_row = jnp.maximum(inter, jnp.max(dlog, axis=1, keepdims=True))
    w_inter = jnp.exp(inter - m_row)
    qb, kb, vb = q.astype(BF16), k.astype(BF16), v.astype(BF16)
    c_old = c_ref[...]
    n_old = n_ref[...]
    qk = lax.dot_general(qb, kb, (((1,), (1,)), ((), ())), preferred_element_type=F32)
    s = qk * jnp.exp(dlog - m_row)
    cq = lax.dot_general(qb, c_old.astype(BF16), (((1,), (1,)), ((), ())), preferred_element_type=F32)
    num = w_inter * cq + jnp.dot(s.astype(BF16), vb, preferred_element_type=F32)
    den = w_inter * jnp.sum(q * n_old, axis=1, keepdims=True) + jnp.sum(s, axis=1, keepdims=True)
    h = num / jnp.maximum(jnp.abs(den), jnp.exp(-m_row))
    vw_t = (v * w_end).T.astype(BF16)
    c_ref[...] = decay * c_old + jnp.dot(vw_t, kb, preferred_element_type=F32)
    n_ref[...] = decay * n_old + jnp.sum(k * w_end, axis=0, keepdims=True)
    m_ref[...] = m_end
    return h


def _mlstm_body(qf_ref, kf_ref, vf_ref, gf_ref, qb_ref, kb_ref, vb_ref, gb_ref, hf_ref, hb_ref,
                cf, nf, mf, cb, nb, mb, *, qscale):
    @pl.when(pl.program_id(2) == 0)
    def _():
        for c_ref, n_ref, m_ref in ((cf, nf, mf), (cb, nb, mb)):
            c_ref[...] = jnp.zeros_like(c_ref)
            n_ref[...] = jnp.zeros_like(n_ref)
            m_ref[...] = jnp.full_like(m_ref, M_EMPTY)

    gf = gf_ref[...]
    gb = gb_ref[...]
    hf_ref[...] = _mlstm_dir(qf_ref[...] * qscale, kf_ref[...], vf_ref[...], gf[:, 0:1], gf[:, 1:2],
                             cf, nf, mf, False)
    hb_ref[...] = _mlstm_dir(qb_ref[...] * qscale, kb_ref[...], vb_ref[...], gb[:, 2:3], gb[:, 3:4],
                             cb, nb, mb, True)


def _mlstm(p, gates, L, q_col, k_col, v_col, d, S):
    B, Lt, _ = p.shape
    H = gates.shape[1]
    nch = Lt // S
    nlat = L // S

    def fwd(t):
        return jnp.where(t < nch - nlat, nlat + t, t - (nch - nlat))

    def bwd(t):
        return nch - 1 - t

    def seg(col, order):
        return pl.BlockSpec((None, S, d), lambda b, h, t: (b, order(t), col + h))

    def gate(order):
        return pl.BlockSpec((None, None, S, 4), lambda b, h, t: (b, h, order(t), 0))

    out_f = pl.BlockSpec((None, S, d), lambda b, h, t: (b, fwd(t), h))
    out_b = pl.BlockSpec((None, S, d), lambda b, h, t: (b, bwd(t), h))
    state = [pltpu.VMEM((d, d), F32), pltpu.VMEM((1, d), F32), pltpu.VMEM((1, 1), F32)]
    blk = 8 * S * d * 4 + 2 * S * V7X_LANES * 4
    return pl.pallas_call(
        functools.partial(_mlstm_body, qscale=d ** -0.5),
        out_shape=[jax.ShapeDtypeStruct((B, Lt, H * d), F32)] * 2,
        grid=(B, H, nch),
        in_specs=[seg(q_col, fwd), seg(k_col, fwd), seg(v_col, fwd), gate(fwd),
                  seg(q_col, bwd), seg(k_col, bwd), seg(v_col, bwd), gate(bwd)],
        out_specs=[out_f, out_b],
        scratch_shapes=state + state,
        name="mlstm",
        compiler_params=pltpu.CompilerParams(dimension_semantics=("parallel", "parallel", "arbitrary"),
                                             vmem_limit_bytes=_vmem_limit(2 * blk + 16 * S * S * 4 + 4 * d * d * 4)),
    )(p, p, p, gates, p, p, p, gates)


def _row_in(src_hbm, dst_ref, sem, b, src_row, dst_row):
    return pltpu.make_async_copy(src_hbm.at[b, pl.ds(src_row, 1), :], dst_ref.at[pl.ds(dst_row, 1), :], sem)


def _row_out(src_ref, dst_hbm, sem, b, src_row, dst_row):
    return pltpu.make_async_copy(src_ref.at[pl.ds(src_row, 1), :], dst_hbm.at[b, pl.ds(dst_row, 1), :], sem)


def _expert_ffn_body(idx_ref, gate_ref, h_hbm, acc_in, w1_ref, w3_ref, w2_ref, acc_hbm, xs_ref, ys_ref,
                     sem_x, sem_y, sem_s, *, rows, tn, unroll):
    del acc_in
    b = pl.program_id(1)

    def gather_x(r, carry):
        _row_in(h_hbm, xs_ref, sem_x, b, idx_ref[0, 0, r], r).start()
        return carry

    def gather_y(r, carry):
        _row_in(acc_hbm, ys_ref, sem_y, b, idx_ref[0, 0, r], r).start()
        return carry

    lax.fori_loop(0, rows, gather_x, 0, unroll=unroll)
    lax.fori_loop(0, rows, gather_y, 0, unroll=unroll)

    def wait_x(r, carry):
        _row_in(h_hbm, xs_ref, sem_x, b, 0, r).wait()
        return carry

    lax.fori_loop(0, rows, wait_x, 0, unroll=unroll)
    xs = xs_ref[...].astype(BF16)
    a = jnp.dot(xs, w1_ref[...], preferred_element_type=F32)
    g = jnp.dot(xs, w3_ref[...], preferred_element_type=F32)
    hid = (jax.nn.silu(a) * g).astype(BF16)

    def wait_y(r, carry):
        _row_in(acc_hbm, ys_ref, sem_y, b, 0, r).wait()
        return carry

    lax.fori_loop(0, rows, wait_y, 0, unroll=unroll)
    gate = gate_ref[...]
    for j in range(ys_ref.shape[1] // tn):
        sl = slice(j * tn, (j + 1) * tn)
        ys_ref[:, sl] += jnp.dot(hid, w2_ref[:, sl], preferred_element_type=F32) * gate

    def scatter(r, carry):
        _row_out(ys_ref, acc_hbm, sem_s, b, r, idx_ref[0, 0, r]).start()
        return carry

    lax.fori_loop(0, rows, scatter, 0, unroll=unroll)

    def wait_s(r, carry):
        _row_out(ys_ref, acc_hbm, sem_s, b, r, 0).wait()
        return carry

    lax.fori_loop(0, rows, wait_s, 0, unroll=unroll)


def _expert_ffn(hm, acc, idx, gate, w1, w3, w2, tn=1024, unroll=8):
    B, _, D = hm.shape
    E, _, F = w1.shape
    R = idx.shape[-1]
    tn = _pick_tile(D, tn)
    unroll = unroll if R % unroll == 0 else 1
    blk = 3 * D * F * 2 + R * V7X_LANES * 4
    return pl.pallas_call(
        functools.partial(_expert_ffn_body, rows=R, tn=tn, unroll=unroll),
        out_shape=jax.ShapeDtypeStruct(acc.shape, F32),
        grid=(E, B),
        in_specs=[pl.BlockSpec((1, 1, R), lambda e, b: (b * E + e, 0, 0), memory_space=pltpu.SMEM),
                  pl.BlockSpec((None, R, 1), lambda e, b: (b * E + e, 0, 0)),
                  pl.BlockSpec(memory_space=pl.ANY),
                  pl.BlockSpec(memory_space=pl.ANY),
                  pl.BlockSpec((None, D, F), lambda e, b: (e, 0, 0)),
                  pl.BlockSpec((None, D, F), lambda e, b: (e, 0, 0)),
                  pl.BlockSpec((None, F, D), lambda e, b: (e, 0, 0))],
        out_specs=pl.BlockSpec(memory_space=pl.ANY),
        input_output_aliases={3: 0},
        scratch_shapes=[pltpu.VMEM((R, D), F32), pltpu.VMEM((R, D), F32),
                        pltpu.SemaphoreType.DMA(()), pltpu.SemaphoreType.DMA(()), pltpu.SemaphoreType.DMA(())],
        name="expert_ffn",
        compiler_params=pltpu.CompilerParams(dimension_semantics=("arbitrary", "arbitrary"),
                                             vmem_limit_bytes=_vmem_limit(2 * blk + R * D * 10 + R * tn * 4
                                                                          + R * F * 10)),
    )(idx, gate, hm, acc, w1, w3, w2)


def _lane_cumsum_excl(x):
    rows, T = x.shape
    r = lax.broadcasted_iota(jnp.int32, (V7X_LANES, V7X_LANES), 0)
    c = lax.broadcasted_iota(jnp.int32, (V7X_LANES, V7X_LANES), 1)
    tri = jnp.where(r <= c, 1.0, 0.0).astype(BF16)
    off = jnp.zeros((rows, 1), F32)
    outs = []
    for g in range(T // V7X_LANES):
        xg = x[:, g * V7X_LANES:(g + 1) * V7X_LANES]
        inc = jnp.dot(xg.astype(BF16), tri, preferred_element_type=F32)
        outs.append(inc - xg + off)
        off = off + inc[:, V7X_LANES - 1:V7X_LANES]
    return jnp.concatenate(outs, axis=1)


def _route_body(aff_ref, table_ref, out_ref, pos_ref, *, cap, chunk):
    aff = aff_ref[...]
    E, T = aff.shape
    bits = lax.bitcast_convert_type(aff, jnp.int32)
    kth = jnp.zeros((E, 1), jnp.int32)
    for bit in range(30, -1, -1):
        cand = kth | (1 << bit)
        cnt = jnp.sum(jnp.where(bits >= cand, 1.0, 0.0), axis=1, keepdims=True)
        kth = jnp.where(cnt >= cap, cand, kth)
    above = jnp.where(bits > kth, 1.0, 0.0)
    tied = jnp.where(bits == kth, 1.0, 0.0)
    need = cap - jnp.sum(above, axis=1, keepdims=True)
    sel = above + tied * jnp.where(_lane_cumsum_excl(tied) < need, 1.0, 0.0)
    pos_ref[...] = jnp.where(sel > 0.0, _lane_cumsum_excl(sel), -1.0)
    slot = lax.broadcasted_iota(jnp.int32, (chunk, T), 0).astype(F32)

    def compact(e, carry):
        pos_e = pos_ref[pl.ds(e, 1), :]
        for r0 in range(0, cap, chunk):
            onehot = jnp.where(slot + float(r0) == pos_e, 1.0, 0.0).astype(BF16)
            out_ref[e, r0:r0 + chunk, :] = jnp.dot(onehot, table_ref[...], preferred_element_type=F32)
        return carry

    lax.fori_loop(0, E, compact, 0)


def _route(aff_t, table, cap):
    B, E, T = aff_t.shape
    chunk = _pick_tile(cap, V7X_LANES, 16)
    blk = E * T * 4 + T * V7X_LANES * 2 + E * cap * V7X_LANES * 4
    return pl.pallas_call(
        functools.partial(_route_body, cap=cap, chunk=chunk),
        out_shape=jax.ShapeDtypeStruct((B, E, cap, V7X_LANES), F32),
        grid=(B,),
        in_specs=[pl.BlockSpec((None, E, T), lambda b: (b, 0, 0)),
                  pl.BlockSpec((None, T, V7X_LANES), lambda b: (b, 0, 0))],
        out_specs=pl.BlockSpec((None, E, cap, V7X_LANES), lambda b: (b, 0, 0, 0)),
        scratch_shapes=[pltpu.VMEM((E, T), F32)],
        name="route",
        compiler_params=pltpu.CompilerParams(dimension_semantics=("parallel",),
                                             vmem_limit_bytes=_vmem_limit(2 * blk + 8 * chunk * T * 4 + 8 * E * T * 4)),
    )(aff_t, table)


def _route_tables(aff):
    B, T, E = aff.shape
    a1 = aff.astype(BF16)
    r1 = aff - a1.astype(F32)
    a2 = r1.astype(BF16)
    a3 = (r1 - a2.astype(F32)).astype(BF16)
    t = jnp.arange(T, dtype=jnp.int32)
    digits = jnp.stack([t // ROUTE_DIGIT, t % ROUTE_DIGIT], axis=-1).astype(BF16)
    digits = jnp.broadcast_to(digits[None], (B, T, 2))
    return _pad_to(jnp.concatenate([a1, a2, a3, digits], axis=-1), 2, V7X_LANES)


def _route_unpack(packed, E, row0):
    parts = packed[..., :3 * E].reshape(packed.shape[:3] + (3, E))
    own = jnp.eye(E, dtype=F32)[None, :, None, None, :]
    gate = jnp.sum(parts * own, axis=(-1, -2))
    idx = jnp.round(packed[..., 3 * E] * ROUTE_DIGIT + packed[..., 3 * E + 1]).astype(jnp.int32) + row0
    return idx, gate


def _short_conv_body(x_ref, w_ref, b_ref, o_ref):
    x = x_ref[...]
    n = x.shape[0]
    row = lax.broadcasted_iota(jnp.int32, x.shape, 0)
    r1 = pltpu.roll(x, 1, axis=0)
    r2 = pltpu.roll(x, n - 1, axis=0)
    one_brings_prev = pltpu.roll(row, 1, axis=0) == jnp.where(row == 0, n - 1, row - 1)
    prev = jnp.where(row == 0, 0.0, jnp.where(one_brings_prev, r1, r2))
    nxt = jnp.where(row == n - 1, 0.0, jnp.where(one_brings_prev, r2, r1))
    w = w_ref[...]
    o_ref[...] = w[0:1] * prev + w[1:2] * x + w[2:3] * nxt + b_ref[...]


def _short_conv(p, col0, width, rows, w, b, tc=128):
    B = p.shape[0]
    r0, n = rows
    tc = _pick_tile(width, tc)
    assert r0 % n == 0 and col0 % tc == 0
    rb, cb = r0 // n, col0 // tc
    return pl.pallas_call(
        _short_conv_body,
        out_shape=jax.ShapeDtypeStruct((B, n, width), F32),
        grid=(B, width // tc),
        in_specs=[pl.BlockSpec((None, n, tc), lambda b_, j: (b_, rb, cb + j)),
                  pl.BlockSpec((w.shape[0], tc), lambda b_, j: (0, j)),
                  pl.BlockSpec((1, tc), lambda b_, j: (0, j))],
        out_specs=pl.BlockSpec((None, n, tc), lambda b_, j: (b_, 0, j)),
        name="short_conv",
        compiler_params=pltpu.CompilerParams(dimension_semantics=("parallel", "parallel"),
                                             vmem_limit_bytes=_vmem_limit(4 * n * tc * 4 + 6 * n * tc * 4)),
    )(p, w, b.reshape(1, width))


def _rope_tables(L, Lc):
    rows = L // GRID_W
    row = jnp.repeat(jnp.arange(rows, dtype=F32), GRID_W)
    col = jnp.tile(jnp.arange(GRID_W, dtype=F32), rows)
    half = HEAD_DIM // 2
    inv = ROPE_THETA ** (-jnp.arange(0, half, 2, dtype=F32) / half)
    ang = jnp.concatenate([row[:, None] * inv, col[:, None] * inv], axis=-1)
    cos, sin = jnp.cos(ang), jnp.sin(ang)
    cos2 = jnp.repeat(cos, 2, axis=-1)
    sin2 = jnp.stack([-sin, sin], axis=-1).reshape(L, HEAD_DIM)
    cos2 = jnp.concatenate([cos2, jnp.ones((Lc, HEAD_DIM), F32)], axis=0)
    sin2 = jnp.concatenate([sin2, jnp.zeros((Lc, HEAD_DIM), F32)], axis=0)
    return cos2, sin2


def _dft_matrices(L):
    N = 2 * L
    k = jnp.arange(L, dtype=jnp.int32)
    n = jnp.arange(L, dtype=jnp.int32)
    if L % DFT_FINE == 0:
        n1 = jnp.arange(L // DFT_FINE, dtype=jnp.int32) * DFT_FINE
        n2 = jnp.arange(DFT_FINE, dtype=jnp.int32)
        a = (2.0 * math.pi / N) * ((k[:, None] * n1[None, :]) % N).astype(F32)
        b = (2.0 * math.pi / N) * ((k[:, None] * n2[None, :]) % N).astype(F32)
        ca, sa, cb, sb = jnp.cos(a)[:, :, None], jnp.sin(a)[:, :, None], jnp.cos(b)[:, None, :], jnp.sin(b)[:, None, :]
        cos = (ca * cb - sa * sb).reshape(L, L)
        sin = (sa * cb + ca * sb).reshape(L, L)
    else:
        ang = (2.0 * math.pi / N) * ((k[:, None] * n[None, :]) % N).astype(F32)
        cos, sin = jnp.cos(ang), jnp.sin(ang)
    nyq = jnp.where(n % 2 == 0, 1.0, -1.0).astype(F32)[None, :]
    first = (k == 0)[:, None]
    fwd = jnp.concatenate([cos, jnp.where(first, nyq, -sin)], axis=0)
    inv_re = jnp.where(first, 1.0, 2.0) * cos / N
    inv_im = jnp.where(first, nyq / N, -2.0 * sin / N)
    inv = jnp.concatenate([inv_re, inv_im], axis=0).T
    return fwd.astype(BF16), inv.astype(BF16)


def _pad_to(a, axis, size):
    pad = [(0, 0)] * a.ndim
    pad[axis] = (0, size - a.shape[axis])
    return jnp.pad(a, pad)


def _hyena_filters(L, w1, b1, w2, b2, w3, sin_freq, width):
    bands = (HY_EMB - 1) // 2
    t = jnp.linspace(0.0, 1.0, L, dtype=F32)[:, None]
    w = (2.0 * math.pi / L) * jnp.arange(L, dtype=F32)[:, None]
    f = jnp.linspace(1e-4, bands - 1, bands, dtype=F32)[None, :]
    z = jnp.concatenate([t, jnp.cos(f * w), -jnp.sin(f * w)], axis=-1)
    ffn = w1.shape[1]
    a = _matmul(_pad_to(z, 1, V7X_LANES), _pad_to(_pad_to(w1, 0, V7X_LANES), 1, V7X_LANES), name="hy_f1")
    a = jnp.sin(sin_freq[0] * (a[:, :ffn] + b1))
    a = _matmul(_pad_to(a, 1, V7X_LANES), _pad_to(_pad_to(w2, 0, V7X_LANES), 1, V7X_LANES), name="hy_f2")
    a = jnp.sin(sin_freq[1] * (a[:, :ffn] + b2))
    h = _matmul(_pad_to(a, 1, V7X_LANES), _pad_to(w3, 0, V7X_LANES), name="hy_f3")
    h = h.reshape(L, HY_ORDER, 2, width)
    rates = jnp.linspace(HY_SLOW_RATE, HY_FAST_RATE, width, dtype=F32)
    h = h * jnp.exp(-t[:, :, None, None] * rates)
    return h * lax.rsqrt(jnp.sum(jnp.square(h), axis=(0, 2), keepdims=True) + EPS)


def _hyena(u, filt, bias, dft):
    B, L, C3 = u.shape
    C = C3 // 3
    fwd, inv = dft
    first = (jnp.arange(L) == 0)[:, None]
    hf = filt[:, :, 0]
    hb = jnp.where(first[:, :, None], 0.0, filt[:, :, 1])
    hsum = jnp.moveaxis(hf + hb, 1, 0)
    hdif = jnp.moveaxis(hf - hb, 1, 0)
    sign = jnp.where(jnp.arange(L) % 2 == 0, 1.0, -1.0).astype(F32)[None, :, None]
    Kr_all = _matmul(fwd[:L], hsum.astype(BF16), name="hy_dft_filter")
    Ki_all = _matmul(fwd[L:], hdif.astype(BF16), name="hy_dft_filter")
    nyq = jnp.sum(hsum * sign, axis=1, keepdims=True)
    Ki_all = jnp.where(first, nyq, Ki_all)
    zb = u[..., :C].astype(BF16)
    z = u
    for o in range(HY_ORDER):
        yr, yi = _dft_fwd_mul(fwd, zb, Kr_all[o], Ki_all[o])
        last = o == HY_ORDER - 1
        res = _dft_inv_gate(inv, yr, yi, u, (1 + o) * C, z, 0, bias[o].reshape(1, C), emit_f32=not last)
        if last:
            return res[0]
        z, zb = res


def _dft_fwd_mul_body(fr_ref, fi_ref, z_ref, kr_ref, ki_ref, yr_ref, yi_ref):
    z = z_ref[...]
    ur = jnp.dot(fr_ref[...], z, preferred_element_type=F32)
    ui = jnp.dot(fi_ref[...], z, preferred_element_type=F32)
    kr, ki = kr_ref[...], ki_ref[...]
    tm = ur.shape[0]
    row = lax.broadcasted_iota(jnp.int32, (tm, 1), 0) + pl.program_id(0) * tm
    first = row == 0
    ii = ui * ki
    yr_ref[...] = (ur * kr - jnp.where(first, 0.0, ii)).astype(yr_ref.dtype)
    yi_ref[...] = jnp.where(first, ii, ur * ki + ui * kr).astype(yi_ref.dtype)


def _dft_fwd_mul(fwd, zb, kr, ki, tm=512, tn=512):
    B, L, C = zb.shape
    tm = _pick_tile(L, tm)
    tn = _pick_tile(C, tn)
    nlo = L // tm
    blk = 2 * tm * L * 2 + L * tn * 2 + 2 * tm * tn * 4 + 2 * tm * tn * 2
    out = jax.ShapeDtypeStruct((B, L, C), BF16)
    return pl.pallas_call(
        _dft_fwd_mul_body,
        out_shape=[out, out],
        grid=(nlo, B, C // tn),
        in_specs=[pl.BlockSpec((tm, L), lambda i, b, j: (i, 0)),
                  pl.BlockSpec((tm, L), lambda i, b, j: (nlo + i, 0)),
                  pl.BlockSpec((None, L, tn), lambda i, b, j: (b, 0, j)),
                  pl.BlockSpec((tm, tn), lambda i, b, j: (i, j)),
                  pl.BlockSpec((tm, tn), lambda i, b, j: (i, j))],
        out_specs=[pl.BlockSpec((None, tm, tn), lambda i, b, j: (b, i, j))] * 2,
        name="hy_dft_fwd",
        compiler_params=pltpu.CompilerParams(dimension_semantics=("parallel", "parallel", "parallel"),
                                             vmem_limit_bytes=_vmem_limit(2 * blk + 4 * tm * tn * 4)),
    )(fwd, fwd, zb, kr, ki)


def _dft_inv_gate_body(ir_ref, ii_ref, yr_ref, yi_ref, g_ref, z_ref, b_ref, *outs, emit_f32):
    conv = (jnp.dot(ir_ref[...], yr_ref[...], preferred_element_type=F32)
            + jnp.dot(ii_ref[...], yi_ref[...], preferred_element_type=F32))
    zn = g_ref[...] * (conv + b_ref[...] * z_ref[...])
    if emit_f32:
        outs[0][...] = zn
    outs[-1][...] = zn.astype(BF16)


def _dft_inv_gate(inv, yr, yi, u, gate_col, z, z_col, bias, emit_f32, tm=512, tn=512):
    B, L, C = yr.shape
    tm = _pick_tile(L, tm)
    tn = _pick_tile(C, tn)
    gj, zj = gate_col // tn, z_col // tn
    tile = pl.BlockSpec((None, tm, tn), lambda i, b, j: (b, i, j))
    spec = pl.BlockSpec((None, L, tn), lambda i, b, j: (b, 0, j))
    blk = 2 * tm * L * 2 + 2 * L * tn * 2 + 3 * tm * tn * 4 + tm * tn * 2
    out_shape = [jax.ShapeDtypeStruct((B, L, C), BF16)]
    if emit_f32:
        out_shape.insert(0, jax.ShapeDtypeStruct((B, L, C), F32))
    return pl.pallas_call(
        functools.partial(_dft_inv_gate_body, emit_f32=emit_f32),
        out_shape=out_shape,
        grid=(L // tm, B, C // tn),
        in_specs=[pl.BlockSpec((tm, L), lambda i, b, j: (i, 0)),
                  pl.BlockSpec((tm, L), lambda i, b, j: (i, 1)),
                  spec, spec,
                  pl.BlockSpec((None, tm, tn), lambda i, b, j: (b, i, gj + j)),
                  pl.BlockSpec((None, tm, tn), lambda i, b, j: (b, i, zj + j)),
                  pl.BlockSpec((1, tn), lambda i, b, j: (0, j))],
        out_specs=[tile] * len(out_shape),
        name="hy_dft_inv",
        compiler_params=pltpu.CompilerParams(dimension_semantics=("parallel", "parallel", "parallel"),
                                             vmem_limit_bytes=_vmem_limit(2 * blk + 2 * tm * tn * 4)),
    )(inv, inv, yr, yi, u, z, bias)


def kernel(x, c, ctx, c_ctx, ada_down, ada_up, ada_b, w_in, w_out, da_lambda, da_subln, gq_qk_norm,
           hy_conv_w, hy_conv_b, hy_f_w1, hy_f_b1, hy_f_w2, hy_f_b2, hy_f_w3, hy_sin_freq, hy_bias,
           ml_gate_b, ml_norm, ln_g, ln_b, router, ex_w1, ex_w3, ex_w2):
    B, L, D = x.shape
    Lc = ctx.shape[1]
    Lt = L + Lc
    depth = w_in.shape[0]
    E = router.shape[-1]
    alpha = (2 * depth) ** 0.25
    da_heads = D // (8 * HEAD_DIM)
    da_v = 2 * HEAD_DIM
    gq_heads = D // (4 * HEAD_DIM)
    gq_kv = gq_heads // 4
    gq_rep = gq_heads // gq_kv
    hy_w = D // 4
    ml_w = D - da_heads * da_v - gq_heads * HEAD_DIM - hy_w
    ml_d = ml_w // ML_HEADS
    widths = [da_heads * 2 * HEAD_DIM, da_heads * 2 * HEAD_DIM, da_heads * da_v, gq_heads * HEAD_DIM,
              gq_kv * HEAD_DIM, gq_kv * HEAD_DIM, hy_w, hy_w, hy_w, ml_w, ml_w, ml_w, ml_w, 4 * ML_HEADS]
    offs = np.concatenate([[0], np.cumsum(widths)]).astype(int)
    o_daq, o_dak, o_dav, o_gqq, o_gqk, o_gqv, o_hy, _, _, o_mlq, o_mlk, o_mlv, o_mlo, o_mlg, n_in = offs
    n_pad = -(-n_in // 512) * 512
    att_w = o_hy
    S = min(256, Lc)
    assert L % S == 0 and Lc % S == 0 and L % Lc == 0

    qs = HEAD_DIM ** -0.5 * math.log2(math.e)
    n_da = 2 * da_heads
    plan = ([(None, True, qs)] * n_da + [(None, True, 1.0)] * n_da + [(None, False, 1.0)] * n_da
            + [(0, True, qs)] * gq_heads + [(1, True, 1.0)] * gq_kv + [(None, False, 1.0)] * gq_kv)
    cos2, sin2 = _rope_tables(L, Lc)
    dft_lat = _dft_matrices(L)
    dft_ctx = _dft_matrices(Lc)

    xs = jnp.concatenate([x, ctx], axis=1)
    cond = jnp.concatenate([c, c_ctx[None]], axis=0)
    cond = _pad_to(jax.nn.silu(cond), 0, V7X_SUBLANES)

    mods_all = []
    for layer in range(depth):
        m = _matmul(_matmul(cond, ada_down[layer], name="ada_down"), ada_up[layer], name="ada_up")
        mods_all.append((m[:B + 1] + ada_b[layer]).reshape(B + 1, N_MOD, D))

    for layer in range(depth):
        last = layer == depth - 1
        mods = mods_all[layer]

        if layer == 0:
            h = _modulate(xs, mods, L, 0, 1)
        w_in_l = _pad_to(w_in[layer], 1, n_pad).astype(BF16)
        p = _matmul(h.reshape(B * Lt, D), w_in_l, name="in_proj").reshape(B, Lt, n_pad)

        a = _qk_prep(p, cos2, sin2, gq_qk_norm[layer], plan)
        lam_init = 0.8 - 0.6 * math.exp(-0.3 * layer)
        lp = da_lambda[layer]
        lam = jnp.exp(jnp.sum(lp[0] * lp[1])) - jnp.exp(jnp.sum(lp[2] * lp[3])) + lam_init
        da = _attention(_da_attn_body, a, L, groups=da_heads, qw=da_v, kw=da_v, vw=da_v, ow=da_v,
                        q_col=o_daq // da_v, k_col=o_dak // da_v, v_col=o_dav // da_v, kv_div=1,
                        nchunk=DA_KEY_CHUNKS, scalars=(lam.reshape(1, 1),),
                        vectors=((da_subln[layer] * (1.0 - lam_init)).reshape(1, da_v),), name="da_attention")
        gq_w = GQ_HEADS_PER_STEP * HEAD_DIM
        gq = _attention(functools.partial(_gq_attn_body, hp=GQ_HEADS_PER_STEP), a, L,
                        groups=gq_heads // GQ_HEADS_PER_STEP, qw=gq_w, kw=HEAD_DIM, vw=HEAD_DIM, ow=gq_w,
                        q_col=o_gqq // gq_w, k_col=o_gqk // HEAD_DIM, v_col=o_gqv // HEAD_DIM,
                        kv_div=gq_rep // GQ_HEADS_PER_STEP, nchunk=GQ_KEY_CHUNKS, tq=GQ_QUERY_TILE,
                        name="gq_attention")

        conv_args = (hy_conv_w[layer], hy_conv_b[layer])
        filt_args = (hy_f_w1[layer], hy_f_b1[layer], hy_f_w2[layer], hy_f_b2[layer], hy_f_w3[layer],
                     hy_sin_freq[layer], hy_w)
        u_lat = _short_conv(p, o_hy, 3 * hy_w, (0, L), *conv_args)
        u_ctx = _short_conv(p, o_hy, 3 * hy_w, (L, Lc), *conv_args)
        hy_lat = _hyena(u_lat, _hyena_filters(L, *filt_args), hy_bias[layer], dft_lat)
        hy_ctx = _hyena(u_ctx, _hyena_filters(Lc, *filt_args), hy_bias[layer], dft_ctx)
        hy = jnp.concatenate([hy_lat, hy_ctx], axis=1)

        g = (p[:, :, o_mlg:o_mlg + 4 * ML_HEADS] + ml_gate_b[layer]).reshape(B, Lt, 4, ML_HEADS)
        gates = jnp.transpose(g, (0, 3, 1, 2))
        hf, hb = _mlstm(p, gates, L, o_mlq // ml_d, o_mlk // ml_d, o_mlv // ml_d, ml_d, S)
        ml = _ml_post(hf, hb, p, o_mlo // ml_d, ml_norm[layer].reshape(ML_HEADS, 1, ml_d))

        slabs = [s.reshape(B * Lt, s.shape[-1]) for s in (da, gq, hy, ml)]
        mix = _matmul_multi(slabs, w_out[layer].astype(BF16), name="out_proj").reshape(B, Lt, D)

        xs, hm = _resid_ln(xs, mix, mods, mods, ln_g[layer, 0], ln_b[layer, 0], L, alpha, 2, 3, 4)

        logits = _matmul(hm.reshape(B * Lt, D), _pad_to(router[layer], 1, V7X_LANES), name="router")
        aff = jax.nn.softmax(logits[:, :E].reshape(B, Lt, E), axis=-1)
        cap, cap_c = CAPACITY_FACTOR * L // E, CAPACITY_FACTOR * Lc // E
        idx_l, gate_l = _route_unpack(_route(jnp.swapaxes(aff[:, :L], 1, 2), _route_tables(aff[:, :L]), cap), E, 0)
        idx_c, gate_c = _route_unpack(_route(jnp.swapaxes(aff[:, L:], 1, 2), _route_tables(aff[:, L:]), cap_c), E, L)
        idx = jnp.concatenate([idx_l, idx_c], axis=-1)
        gate = jnp.concatenate([gate_l, gate_c], axis=-1)
        R = cap + cap_c
        ffn = _expert_ffn(hm, jnp.zeros((B, Lt, D), F32), idx.reshape(B * E, 1, R), gate.reshape(B * E, R, 1),
                          ex_w1[layer].astype(BF16), ex_w3[layer].astype(BF16), ex_w2[layer].astype(BF16))

        if last:
            xs, _ = _resid_ln(xs, ffn, mods, mods, ln_g[layer, 1], ln_b[layer, 1], L, alpha, 5, rows=L)
        else:
            xs, h = _resid_ln(xs, ffn, mods, mods_all[layer + 1], ln_g[layer, 1], ln_b[layer, 1], L, alpha, 5, 0, 1,
                              h_dtype=BF16)
    return xs
```

```python
import functools
import math

import jax
import jax.numpy as jnp
import numpy as np
from jax import lax
from jax.experimental import pallas as pl
from jax.experimental.pallas import tpu as pltpu

HEAD_DIM = 128
GRID_W = 64
ROPE_THETA = 10000.0
HY_ORDER = 2
HY_EMB = 33
HY_SLOW_RATE = -math.log(1e-2) / 1.5
HY_FAST_RATE = -math.log(1e-2) / 0.3
ML_HEADS = 4
M_EMPTY = -1e30
CAPACITY_FACTOR = 2
N_MOD = 6
GQ_HEADS_PER_STEP = 2
GQ_QUERY_TILE = 256
GQ_KEY_CHUNKS = 1
DA_KEY_CHUNKS = 2
ATTN_MIN_CHUNKED_KEYS = 1024
DFT_FINE = 64
ROUTE_DIGIT = 64
EPS = 1e-6
LN_EPS = 1e-5

V7X_LANES = 128
V7X_SUBLANES = 8
V7X_VMEM_BYTES = 64 * 1024 * 1024
V7X_VMEM_REQUEST_CAP = 56 * 1024 * 1024
MIN_VMEM_REQUEST = 32 * 1024 * 1024

BF16 = jnp.bfloat16
F32 = jnp.float32


def _pick_tile(n, pref, mult=V7X_LANES):
    if n <= pref:
        return n
    t = (pref // mult) * mult
    while t >= mult:
        if n % t == 0:
            return t
        t -= mult
    return n


def _vmem_limit(block_bytes):
    return int(min(max(block_bytes * 5 // 4, MIN_VMEM_REQUEST), V7X_VMEM_REQUEST_CAP))


def _mm_body(a_ref, b_ref, o_ref, *acc, nk, kaxis):
    prod = jnp.dot(a_ref[...].astype(BF16), b_ref[...].astype(BF16), preferred_element_type=F32)
    if nk == 1:
        o_ref[...] = prod.astype(o_ref.dtype)
        return
    (acc_ref,) = acc
    k = pl.program_id(kaxis)

    @pl.when(k == 0)
    def _():
        acc_ref[...] = prod

    @pl.when(k > 0)
    def _():
        acc_ref[...] += prod

    @pl.when(k == nk - 1)
    def _():
        o_ref[...] = acc_ref[...].astype(o_ref.dtype)


def _matmul(a, b, out_dtype=F32, tm=1024, tn=512, tk=4096, name="matmul"):
    M, K = a.shape
    batched = b.ndim == 3
    N = b.shape[-1]
    assert b.shape[-2] == K
    tm = _pick_tile(M, tm, V7X_SUBLANES)
    tn = _pick_tile(N, tn)
    tk = _pick_tile(K, tk)
    nk = K // tk
    abytes = tm * tk * a.dtype.itemsize
    bbytes = tk * tn * b.dtype.itemsize
    obytes = tm * tn * jnp.dtype(out_dtype).itemsize
    vmem = _vmem_limit(2 * (abytes + bbytes + obytes) + (tm * tn * 4 if nk > 1 else 0)
                       + tm * tn * 4 + (tm * tk + tk * tn) * 2)
    scratch = [pltpu.VMEM((tm, tn), F32)] if nk > 1 else []
    if batched:
        G = b.shape[0]
        grid = (M // tm, G, N // tn, nk)
        in_specs = [pl.BlockSpec((tm, tk), lambda i, g, j, k: (i, k)),
                    pl.BlockSpec((None, tk, tn), lambda i, g, j, k: (g, k, j))]
        out_specs = pl.BlockSpec((None, tm, tn), lambda i, g, j, k: (g, i, j))
        out_shape = jax.ShapeDtypeStruct((G, M, N), out_dtype)
        sem = ("parallel", "parallel", "parallel", "arbitrary")
        kaxis = 3
    else:
        grid = (M // tm, N // tn, nk)
        in_specs = [pl.BlockSpec((tm, tk), lambda i, j, k: (i, k)),
                    pl.BlockSpec((tk, tn), lambda i, j, k: (k, j))]
        out_specs = pl.BlockSpec((tm, tn), lambda i, j, k: (i, j))
        out_shape = jax.ShapeDtypeStruct((M, N), out_dtype)
        sem = ("parallel", "parallel", "arbitrary")
        kaxis = 2
    return pl.pallas_call(
        functools.partial(_mm_body, nk=nk, kaxis=kaxis),
        out_shape=out_shape, grid=grid, in_specs=in_specs, out_specs=out_specs,
        scratch_shapes=scratch, name=name,
        compiler_params=pltpu.CompilerParams(dimension_semantics=sem, vmem_limit_bytes=vmem),
    )(a, b)


def _mod_index(b, i, nlat, nb):
    return jnp.where(i < nlat, b, nb)


def _modulate_body(x_ref, m_ref, h_ref, *, shift, scale):
    m = m_ref[...]
    h_ref[...] = (x_ref[...] * (1.0 + m[scale:scale + 1]) + m[shift:shift + 1]).astype(h_ref.dtype)


def _modulate(x, mods, L, shift, scale, out_dtype=BF16, tr=256):
    B, Lt, D = x.shape
    tr = _pick_tile(math.gcd(L, Lt - L), tr, V7X_SUBLANES)
    nlat = L // tr
    blk = tr * D * (4 + jnp.dtype(out_dtype).itemsize)
    return pl.pallas_call(
        functools.partial(_modulate_body, shift=shift, scale=scale),
        out_shape=jax.ShapeDtypeStruct((B, Lt, D), out_dtype),
        grid=(B, Lt // tr),
        in_specs=[pl.BlockSpec((None, tr, D), lambda b, i: (b, i, 0)),
                  pl.BlockSpec((None, N_MOD, D), lambda b, i: (_mod_index(b, i, nlat, B), 0, 0))],
        out_specs=pl.BlockSpec((None, tr, D), lambda b, i: (b, i, 0)),
        name="modulate",
        compiler_params=pltpu.CompilerParams(dimension_semantics=("parallel", "parallel"),
                                             vmem_limit_bytes=_vmem_limit(2 * blk)),
    )(x, mods)


def _resid_ln_body(x_ref, y_ref, m_ref, m2_ref, g_ref, b_ref, xo_ref, *h_ref, alpha, gate, shift, scale):
    m = m_ref[...]
    v = alpha * x_ref[...] + m[gate:gate + 1] * y_ref[...]
    mu = jnp.mean(v, axis=-1, keepdims=True)
    vc = v - mu
    var = jnp.mean(vc * vc, axis=-1, keepdims=True)
    out = vc * lax.rsqrt(var + LN_EPS) * g_ref[...] + b_ref[...]
    xo_ref[...] = out
    if h_ref:
        m2 = m2_ref[...]
        h_ref[0][...] = (out * (1.0 + m2[scale:scale + 1]) + m2[shift:shift + 1]).astype(h_ref[0].dtype)


def _resid_ln(x, y, mods, mods2, g, b, L, alpha, gate, shift=None, scale=None, h_dtype=F32, rows=None, tr=256):
    B, Lt, D = x.shape
    rows = Lt if rows is None else rows
    tr = _pick_tile(math.gcd(L, Lt - L), tr, V7X_SUBLANES)
    nlat = L // tr
    emit_h = shift is not None
    row = pl.BlockSpec((None, tr, D), lambda b_, i: (b_, i, 0))
    modspec = pl.BlockSpec((None, N_MOD, D), lambda b_, i: (_mod_index(b_, i, nlat, B), 0, 0))
    vec = pl.BlockSpec((1, D), lambda b_, i: (0, 0))
    out_shape = [jax.ShapeDtypeStruct((B, rows, D), F32)]
    out_specs = [row]
    if emit_h:
        out_shape.append(jax.ShapeDtypeStruct((B, rows, D), h_dtype))
        out_specs.append(row)
    blk = tr * D * 4 * 4
    res = pl.pallas_call(
        functools.partial(_resid_ln_body, alpha=alpha, gate=gate, shift=shift, scale=scale),
        out_shape=out_shape, grid=(B, rows // tr),
        in_specs=[row, row, modspec, modspec, vec, vec], out_specs=out_specs,
        name="resid_ln",
        compiler_params=pltpu.CompilerParams(dimension_semantics=("parallel", "parallel"),
                                             vmem_limit_bytes=_vmem_limit(2 * blk)),
    )(x, y, mods, mods2, g.reshape(1, D), b.reshape(1, D))
    return (res[0], res[1]) if emit_h else (res[0], None)


def _qk_prep_body(p_ref, cos_ref, sin_ref, gain_ref, o_ref, *, plan):
    cos = cos_ref[...]
    sin = sin_ref[...]
    tr = cos.shape[0]
    lane = lax.broadcasted_iota(jnp.int32, (tr, HEAD_DIM), 1)
    partner = lane ^ 1
    from_prev = pltpu.roll(lane, 1, axis=1) == partner
    for blk, (norm_row, rope, post) in enumerate(plan):
        sl = slice(blk * HEAD_DIM, (blk + 1) * HEAD_DIM)
        v = p_ref[:, sl]
        if norm_row is not None:
            ms = jnp.mean(v * v, axis=-1, keepdims=True)
            v = v * lax.rsqrt(ms + EPS) * gain_ref[norm_row:norm_row + 1, :]
        if rope:
            swapped = jnp.where(from_prev, pltpu.roll(v, 1, axis=1), pltpu.roll(v, HEAD_DIM - 1, axis=1))
            v = v * cos + swapped * sin
        if post != 1.0:
            v = v * post
        o_ref[:, sl] = v.astype(o_ref.dtype)


def _qk_prep(p, cos, sin, gains, plan, tr=256):
    B, Lt, _ = p.shape
    W = len(plan) * HEAD_DIM
    tr = _pick_tile(Lt, tr, 16)
    blk = tr * W * (4 + 2) + 2 * tr * HEAD_DIM * 4
    return pl.pallas_call(
        functools.partial(_qk_prep_body, plan=tuple(plan)),
        out_shape=jax.ShapeDtypeStruct((B, Lt, W), BF16),
        grid=(B, Lt // tr),
        in_specs=[pl.BlockSpec((None, tr, W), lambda b, i: (b, i, 0)),
                  pl.BlockSpec((tr, HEAD_DIM), lambda b, i: (i, 0)),
                  pl.BlockSpec((tr, HEAD_DIM), lambda b, i: (i, 0)),
                  pl.BlockSpec(gains.shape, lambda b, i: (0, 0))],
        out_specs=pl.BlockSpec((None, tr, W), lambda b, i: (b, i, 0)),
        name="qk_prep",
        compiler_params=pltpu.CompilerParams(dimension_semantics=("parallel", "parallel"),
                                             vmem_limit_bytes=_vmem_limit(2 * blk)),
    )(p, cos, sin, gains)


def _softmax_pv(q, k_ref, v_ref, lo, kcols, nchunk):
    hi = k_ref.shape[0]
    if (hi - lo) < ATTN_MIN_CHUNKED_KEYS or (hi - lo) % (16 * nchunk) != 0:
        nchunk = 1
    n = (hi - lo) // nchunk
    parts = []
    for c in range(nchunk):
        rows = slice(lo + c * n, lo + (c + 1) * n)
        s = lax.dot_general(q, k_ref[rows, kcols], (((1,), (1,)), ((), ())), preferred_element_type=F32)
        m = jnp.max(s, axis=-1, keepdims=True)
        e = jnp.exp2(s - m)
        l = jnp.sum(e, axis=-1, keepdims=True)
        parts.append((m, l, jnp.dot(e.astype(BF16), v_ref[rows, :], preferred_element_type=F32)))
    m, l, o = parts[0]
    for m2, l2, o2 in parts[1:]:
        mn = jnp.maximum(m, m2)
        a1, a2 = jnp.exp2(m - mn), jnp.exp2(m2 - mn)
        m, l, o = mn, l * a1 + l2 * a2, o * a1 + o2 * a2
    return o / l


def _key_ranges(fn, nlat, L):
    tile = pl.program_id(2)

    @pl.when(tile < nlat)
    def _():
        fn(0)

    @pl.when(tile >= nlat)
    def _():
        fn(L)


def _da_attn_body(lam_ref, q_ref, k_ref, v_ref, g_ref, o_ref, *, nlat, L, nchunk):
    def run(lo):
        o0 = _softmax_pv(q_ref[:, :HEAD_DIM], k_ref, v_ref, lo, slice(0, HEAD_DIM), nchunk)
        o1 = _softmax_pv(q_ref[:, HEAD_DIM:], k_ref, v_ref, lo, slice(HEAD_DIM, 2 * HEAD_DIM), nchunk)
        d = o0 - lam_ref[0, 0] * o1
        ms = jnp.mean(d * d, axis=-1, keepdims=True)
        o_ref[...] = (d * lax.rsqrt(ms + EPS) * g_ref[...]).astype(o_ref.dtype)

    _key_ranges(run, nlat, L)


def _gq_attn_body(q_ref, k_ref, v_ref, o_ref, *, hp, nlat, L, nchunk):
    def run(lo):
        for j in range(hp):
            sl = slice(j * HEAD_DIM, (j + 1) * HEAD_DIM)
            o_ref[:, sl] = _softmax_pv(q_ref[:, sl], k_ref, v_ref, lo, slice(0, HEAD_DIM), nchunk).astype(o_ref.dtype)

    _key_ranges(run, nlat, L)


def _attention(body, a, L, groups, qw, kw, vw, ow, q_col, k_col, v_col, kv_div, nchunk, scalars=(), vectors=(),
               tq=256, name="attention"):
    B, Lt, _ = a.shape
    tq = _pick_tile(math.gcd(L, Lt - L), tq, 16)
    in_specs = [pl.BlockSpec(memory_space=pltpu.SMEM) for _ in scalars]
    in_specs += [pl.BlockSpec((None, tq, qw), lambda b, h, i: (b, i, q_col + h)),
                 pl.BlockSpec((None, Lt, kw), lambda b, h, i: (b, 0, k_col + h // kv_div)),
                 pl.BlockSpec((None, Lt, vw), lambda b, h, i: (b, 0, v_col + h // kv_div))]
    in_specs += [pl.BlockSpec(vec.shape, lambda b, h, i: (0, 0)) for vec in vectors]
    nprob = max(qw // HEAD_DIM, 1)
    blk = tq * qw * 2 + Lt * (kw + vw) * 2 + tq * ow * 2
    inter = nprob * tq * Lt * (4 + 4 + 2)
    return pl.pallas_call(
        functools.partial(body, nlat=L // tq, L=L, nchunk=nchunk),
        out_shape=jax.ShapeDtypeStruct((B, Lt, groups * ow), BF16),
        grid=(B, groups, Lt // tq),
        in_specs=in_specs,
        out_specs=pl.BlockSpec((None, tq, ow), lambda b, h, i: (b, i, h)),
        name=name,
        compiler_params=pltpu.CompilerParams(dimension_semantics=("parallel", "parallel", "parallel"),
                                             vmem_limit_bytes=_vmem_limit(2 * blk + inter)),
    )(*scalars, a, a, a, *vectors)


def _mm_multi_body(*refs, ks):
    n = len(ks)
    b_ref, o_ref = refs[n], refs[n + 1]
    acc, off = None, 0
    for a_ref, kk in zip(refs[:n], ks):
        part = jnp.dot(a_ref[...].astype(BF16), b_ref[off:off + kk, :], preferred_element_type=F32)
        acc = part if acc is None else acc + part
        off += kk
    o_ref[...] = acc.astype(o_ref.dtype)


def _matmul_multi(a_list, b, out_dtype=F32, tm=1024, tn=512, name="matmul_multi"):
    M = a_list[0].shape[0]
    ks = tuple(a.shape[1] for a in a_list)
    K, N = b.shape
    assert sum(ks) == K
    tm = _pick_tile(M, tm, V7X_SUBLANES)
    tn = _pick_tile(N, tn)
    blk = sum(tm * kk * a.dtype.itemsize for a, kk in zip(a_list, ks)) + K * tn * b.dtype.itemsize + tm * tn * 4
    return pl.pallas_call(
        functools.partial(_mm_multi_body, ks=ks),
        out_shape=jax.ShapeDtypeStruct((M, N), out_dtype),
        grid=(M // tm, N // tn),
        in_specs=[pl.BlockSpec((tm, kk), lambda i, j: (i, 0)) for kk in ks]
        + [pl.BlockSpec((K, tn), lambda i, j: (0, j))],
        out_specs=pl.BlockSpec((tm, tn), lambda i, j: (i, j)),
        name=name,
        compiler_params=pltpu.CompilerParams(dimension_semantics=("parallel", "parallel"),
                                             vmem_limit_bytes=_vmem_limit(2 * blk + 2 * tm * tn * 4)),
    )(*a_list, b)


def _ml_post_body(hf_ref, hb_ref, *rest, heads):
    o_refs, g_ref, out_ref = rest[:heads], rest[heads], rest[heads + 1]
    d = g_ref.shape[-1]
    for h in range(heads):
        sl = slice(h * d, (h + 1) * d)
        hs = hf_ref[:, sl] + hb_ref[:, sl]
        ms = jnp.mean(hs * hs, axis=-1, keepdims=True)
        hn = hs * lax.rsqrt(ms + EPS) * g_ref[h]
        out_ref[:, sl] = (jax.nn.sigmoid(o_refs[h][...]) * hn).astype(out_ref.dtype)


def _ml_post(hf, hb, p, o_col, gains, tr=256):
    B, Lt, W = hf.shape
    H, _, d = gains.shape
    tr = _pick_tile(Lt, tr, 16)
    full = pl.BlockSpec((None, tr, W), lambda b, i: (b, i, 0))
    blk = tr * W * (4 * 3 + 2)
    return pl.pallas_call(
        functools.partial(_ml_post_body, heads=H),
        out_shape=jax.ShapeDtypeStruct((B, Lt, W), BF16),
        grid=(B, Lt // tr),
        in_specs=[full, full]
        + [pl.BlockSpec((None, tr, d), lambda b, i, h=h: (b, i, o_col + h)) for h in range(H)]
        + [pl.BlockSpec((H, 1, d), lambda b, i: (0, 0, 0))],
        out_specs=full,
        name="ml_post",
        compiler_params=pltpu.CompilerParams(dimension_semantics=("parallel", "parallel"),
                                             vmem_limit_bytes=_vmem_limit(2 * blk)),
    )(hf, hb, *([p] * H), gains)


def _mlstm_dir(q, k, v, ig, fpre, c_ref, n_ref, m_ref, reverse):
    S = q.shape[0]
    lf = jax.nn.log_sigmoid(fpre)
    r = lax.broadcasted_iota(jnp.int32, (S, S), 0)
    c = lax.broadcasted_iota(jnp.int32, (S, S), 1)
    eye = r == c
    seen = (c >= r) if reverse else (c <= r)
    seen_t = (r >= c) if reverse else (r <= c)
    zero = jnp.zeros((S, S), F32)
    lf_row = jnp.sum(jnp.where(eye, lf, zero), axis=0, keepdims=True)
    ig_row = jnp.sum(jnp.where(eye, ig, zero), axis=0, keepdims=True)
    b_col = jnp.sum(jnp.where(seen, lf_row, zero), axis=1, keepdims=True)
    b_row = jnp.sum(jnp.where(seen_t, lf, zero), axis=0, keepdims=True)
    total = jnp.sum(lf, axis=0, keepdims=True)
    m_prev = m_ref[...]
    g_row = total - b_row + ig_row
    g_col = total - b_col + ig
    m_end = jnp.maximum(total + m_prev, jnp.max(g_row, axis=1, keepdims=True))
    w_end = jnp.exp(g_col - m_end)
    decay = jnp.exp(total + m_prev - m_end)
    dlog = jnp.where(seen, b_col - b_row + ig_row, -jnp.inf)
    inter = b_col + m_prev
    m_row = jnp.maximum(inter, jnp.max(dlog, axis=1, keepdims=True))
    w_inter = jnp.exp(inter - m_row)
    qb, kb, vb = q.astype(BF16), k.astype(BF16), v.astype(BF16)
    c_old = c_ref[...]
    n_old = n_ref[...]
    qk = lax.dot_general(qb, kb, (((1,), (1,)), ((), ())), preferred_element_type=F32)
    s = qk * jnp.exp(dlog - m_row)
    cq = lax.dot_general(qb, c_old.astype(BF16), (((1,), (1,)), ((), ())), preferred_element_type=F32)
    num = w_inter * cq + jnp.dot(s.astype(BF16), vb, preferred_element_type=F32)
    den = w_inter * jnp.sum(q * n_old, axis=1, keepdims=True) + jnp.sum(s, axis=1, keepdims=True)
    h = num / jnp.maximum(jnp.abs(den), jnp.exp(-m_row))
    vw_t = (v * w_end).T.astype(BF16)
    c_ref[...] = decay * c_old + jnp.dot(vw_t, kb, preferred_element_type=F32)
    n_ref[...] = decay * n_old + jnp.sum(k * w_end, axis=0, keepdims=True)
    m_ref[...] = m_end
    return h


def _mlstm_body(qf_ref, kf_ref, vf_ref, gf_ref, qb_ref, kb_ref, vb_ref, gb_ref, hf_ref, hb_ref,
                cf, nf, mf, cb, nb, mb, *, qscale):
    @pl.when(pl.program_id(2) == 0)
    def _():
        for c_ref, n_ref, m_ref in ((cf, nf, mf), (cb, nb, mb)):
            c_ref[...] = jnp.zeros_like(c_ref)
            n_ref[...] = jnp.zeros_like(n_ref)
            m_ref[...] = jnp.full_like(m_ref, M_EMPTY)

    gf = gf_ref[...]
    gb = gb_ref[...]
    hf_ref[...] = _mlstm_dir(qf_ref[...] * qscale, kf_ref[...], vf_ref[...], gf[:, 0:1], gf[:, 1:2],
                             cf, nf, mf, False)
    hb_ref[...] = _mlstm_dir(qb_ref[...] * qscale, kb_ref[...], vb_ref[...], gb[:, 2:3], gb[:, 3:4],
                             cb, nb, mb, True)


def _mlstm(p, gates, L, q_col, k_col, v_col, d, S):
    B, Lt, _ = p.shape
    H = gates.shape[1]
    nch = Lt // S
    nlat = L // S

    def fwd(t):
        return jnp.where(t < nch - nlat, nlat + t, t - (nch - nlat))

    def bwd(t):
        return nch - 1 - t

    def seg(col, order):
        return pl.BlockSpec((None, S, d), lambda b, h, t: (b, order(t), col + h))

    def gate(order):
        return pl.BlockSpec((None, None, S, 4), lambda b, h, t: (b, h, order(t), 0))

    out_f = pl.BlockSpec((None, S, d), lambda b, h, t: (b, fwd(t), h))
    out_b = pl.BlockSpec((None, S, d), lambda b, h, t: (b, bwd(t), h))
    state = [pltpu.VMEM((d, d), F32), pltpu.VMEM((1, d), F32), pltpu.VMEM((1, 1), F32)]
    blk = 8 * S * d * 4 + 2 * S * V7X_LANES * 4
    return pl.pallas_call(
        functools.partial(_mlstm_body, qscale=d ** -0.5),
        out_shape=[jax.ShapeDtypeStruct((B, Lt, H * d), F32)] * 2,
        grid=(B, H, nch),
        in_specs=[seg(q_col, fwd), seg(k_col, fwd), seg(v_col, fwd), gate(fwd),
                  seg(q_col, bwd), seg(k_col, bwd), seg(v_col, bwd), gate(bwd)],
        out_specs=[out_f, out_b],
        scratch_shapes=state + state,
        name="mlstm",
        compiler_params=pltpu.CompilerParams(dimension_semantics=("parallel", "parallel", "arbitrary"),
                                             vmem_limit_bytes=_vmem_limit(2 * blk + 16 * S * S * 4 + 4 * d * d * 4)),
    )(p, p, p, gates, p, p, p, gates)


def _row_in(src_hbm, dst_ref, sem, b, src_row, dst_row):
    return pltpu.make_async_copy(src_hbm.at[b, pl.ds(src_row, 1), :], dst_ref.at[pl.ds(dst_row, 1), :], sem)


def _row_out(src_ref, dst_hbm, sem, b, src_row, dst_row):
    return pltpu.make_async_copy(src_ref.at[pl.ds(src_row, 1), :], dst_hbm.at[b, pl.ds(dst_row, 1), :], sem)


def _expert_ffn_body(idx_ref, gate_ref, h_hbm, acc_in, w1_ref, w3_ref, w2_ref, acc_hbm, xs_ref, ys_ref,
                     sem_x, sem_y, sem_s, *, rows, tn, unroll):
    del acc_in
    b = pl.program_id(1)

    def gather_x(r, carry):
        _row_in(h_hbm, xs_ref, sem_x, b, idx_ref[0, 0, r], r).start()
        return carry

    def gather_y(r, carry):
        _row_in(acc_hbm, ys_ref, sem_y, b, idx_ref[0, 0, r], r).start()
        return carry

    lax.fori_loop(0, rows, gather_x, 0, unroll=unroll)
    lax.fori_loop(0, rows, gather_y, 0, unroll=unroll)

    def wait_x(r, carry):
        _row_in(h_hbm, xs_ref, sem_x, b, 0, r).wait()
        return carry

    lax.fori_loop(0, rows, wait_x, 0, unroll=unroll)
    xs = xs_ref[...].astype(BF16)
    a = jnp.dot(xs, w1_ref[...], preferred_element_type=F32)
    g = jnp.dot(xs, w3_ref[...], preferred_element_type=F32)
    hid = (jax.nn.silu(a) * g).astype(BF16)

    def wait_y(r, carry):
        _row_in(acc_hbm, ys_ref, sem_y, b, 0, r).wait()
        return carry

    lax.fori_loop(0, rows, wait_y, 0, unroll=unroll)
    gate = gate_ref[...]
    for j in range(ys_ref.shape[1] // tn):
        sl = slice(j * tn, (j + 1) * tn)
        ys_ref[:, sl] += jnp.dot(hid, w2_ref[:, sl], preferred_element_type=F32) * gate

    def scatter(r, carry):
        _row_out(ys_ref, acc_hbm, sem_s, b, r, idx_ref[0, 0, r]).start()
        return carry

    lax.fori_loop(0, rows, scatter, 0, unroll=unroll)

    def wait_s(r, carry):
        _row_out(ys_ref, acc_hbm, sem_s, b, r, 0).wait()
        return carry

    lax.fori_loop(0, rows, wait_s, 0, unroll=unroll)


def _expert_ffn(hm, acc, idx, gate, w1, w3, w2, tn=1024, unroll=8):
    B, _, D = hm.shape
    E, _, F = w1.shape
    R = idx.shape[-1]
    tn = _pick_tile(D, tn)
    unroll = unroll if R % unroll == 0 else 1
    blk = 3 * D * F * 2 + R * V7X_LANES * 4
    return pl.pallas_call(
        functools.partial(_expert_ffn_body, rows=R, tn=tn, unroll=unroll),
        out_shape=jax.ShapeDtypeStruct(acc.shape, F32),
        grid=(E, B),
        in_specs=[pl.BlockSpec((1, 1, R), lambda e, b: (b * E + e, 0, 0), memory_space=pltpu.SMEM),
                  pl.BlockSpec((None, R, 1), lambda e, b: (b * E + e, 0, 0)),
                  pl.BlockSpec(memory_space=pl.ANY),
                  pl.BlockSpec(memory_space=pl.ANY),
                  pl.BlockSpec((None, D, F), lambda e, b: (e, 0, 0)),
                  pl.BlockSpec((None, D, F), lambda e, b: (e, 0, 0)),
                  pl.BlockSpec((None, F, D), lambda e, b: (e, 0, 0))],
        out_specs=pl.BlockSpec(memory_space=pl.ANY),
        input_output_aliases={3: 0},
        scratch_shapes=[pltpu.VMEM((R, D), F32), pltpu.VMEM((R, D), F32),
                        pltpu.SemaphoreType.DMA(()), pltpu.SemaphoreType.DMA(()), pltpu.SemaphoreType.DMA(())],
        name="expert_ffn",
        compiler_params=pltpu.CompilerParams(dimension_semantics=("arbitrary", "arbitrary"),
                                             vmem_limit_bytes=_vmem_limit(2 * blk + R * D * 10 + R * tn * 4
                                                                          + R * F * 10)),
    )(idx, gate, hm, acc, w1, w3, w2)


def _lane_cumsum_excl(x):
    rows, T = x.shape
    r = lax.broadcasted_iota(jnp.int32, (V7X_LANES, V7X_LANES), 0)
    c = lax.broadcasted_iota(jnp.int32, (V7X_LANES, V7X_LANES), 1)
    tri = jnp.where(r <= c, 1.0, 0.0).astype(BF16)
    off = jnp.zeros((rows, 1), F32)
    outs = []
    for g in range(T // V7X_LANES):
        xg = x[:, g * V7X_LANES:(g + 1) * V7X_LANES]
        inc = jnp.dot(xg.astype(BF16), tri, preferred_element_type=F32)
        outs.append(inc - xg + off)
        off = off + inc[:, V7X_LANES - 1:V7X_LANES]
    return jnp.concatenate(outs, axis=1)


def _route_body(aff_ref, table_ref, out_ref, pos_ref, *, cap, chunk):
    aff = aff_ref[...]
    E, T = aff.shape
    bits = lax.bitcast_convert_type(aff, jnp.int32)
    kth = jnp.zeros((E, 1), jnp.int32)
    for bit in range(30, -1, -1):
        cand = kth | (1 << bit)
        cnt = jnp.sum(jnp.where(bits >= cand, 1.0, 0.0), axis=1, keepdims=True)
        kth = jnp.where(cnt >= cap, cand, kth)
    above = jnp.where(bits > kth, 1.0, 0.0)
    tied = jnp.where(bits == kth, 1.0, 0.0)
    need = cap - jnp.sum(above, axis=1, keepdims=True)
    sel = above + tied * jnp.where(_lane_cumsum_excl(tied) < need, 1.0, 0.0)
    pos_ref[...] = jnp.where(sel > 0.0, _lane_cumsum_excl(sel), -1.0)
    slot = lax.broadcasted_iota(jnp.int32, (chunk, T), 0).astype(F32)

    def compact(e, carry):
        pos_e = pos_ref[pl.ds(e, 1), :]
        for r0 in range(0, cap, chunk):
            onehot = jnp.where(slot + float(r0) == pos_e, 1.0, 0.0).astype(BF16)
            out_ref[e, r0:r0 + chunk, :] = jnp.dot(onehot, table_ref[...], preferred_element_type=F32)
        return carry

    lax.fori_loop(0, E, compact, 0)


def _route(aff_t, table, cap):
    B, E, T = aff_t.shape
    chunk = _pick_tile(cap, V7X_LANES, 16)
    blk = E * T * 4 + T * V7X_LANES * 2 + E * cap * V7X_LANES * 4
    return pl.pallas_call(
        functools.partial(_route_body, cap=cap, chunk=chunk),
        out_shape=jax.ShapeDtypeStruct((B, E, cap, V7X_LANES), F32),
        grid=(B,),
        in_specs=[pl.BlockSpec((None, E, T), lambda b: (b, 0, 0)),
                  pl.BlockSpec((None, T, V7X_LANES), lambda b: (b, 0, 0))],
        out_specs=pl.BlockSpec((None, E, cap, V7X_LANES), lambda b: (b, 0, 0, 0)),
        scratch_shapes=[pltpu.VMEM((E, T), F32)],
        name="route",
        compiler_params=pltpu.CompilerParams(dimension_semantics=("parallel",),
                                             vmem_limit_bytes=_vmem_limit(2 * blk + 8 * chunk * T * 4 + 8 * E * T * 4)),
    )(aff_t, table)


def _route_tables(aff):
    B, T, E = aff.shape
    a1 = aff.astype(BF16)
    r1 = aff - a1.astype(F32)
    a2 = r1.astype(BF16)
    a3 = (r1 - a2.astype(F32)).astype(BF16)
    t = jnp.arange(T, dtype=jnp.int32)
    digits = jnp.stack([t // ROUTE_DIGIT, t % ROUTE_DIGIT], axis=-1).astype(BF16)
    digits = jnp.broadcast_to(digits[None], (B, T, 2))
    return _pad_to(jnp.concatenate([a1, a2, a3, digits], axis=-1), 2, V7X_LANES)


def _route_unpack(packed, E, row0):
    parts = packed[..., :3 * E].reshape(packed.shape[:3] + (3, E))
    own = jnp.eye(E, dtype=F32)[None, :, None, None, :]
    gate = jnp.sum(parts * own, axis=(-1, -2))
    idx = jnp.round(packed[..., 3 * E] * ROUTE_DIGIT + packed[..., 3 * E + 1]).astype(jnp.int32) + row0
    return idx, gate


def _short_conv_body(x_ref, w_ref, b_ref, o_ref):
    x = x_ref[...]
    n = x.shape[0]
    row = lax.broadcasted_iota(jnp.int32, x.shape, 0)
    r1 = pltpu.roll(x, 1, axis=0)
    r2 = pltpu.roll(x, n - 1, axis=0)
    one_brings_prev = pltpu.roll(row, 1, axis=0) == jnp.where(row == 0, n - 1, row - 1)
    prev = jnp.where(row == 0, 0.0, jnp.where(one_brings_prev, r1, r2))
    nxt = jnp.where(row == n - 1, 0.0, jnp.where(one_brings_prev, r2, r1))
    w = w_ref[...]
    o_ref[...] = w[0:1] * prev + w[1:2] * x + w[2:3] * nxt + b_ref[...]


def _short_conv(p, col0, width, rows, w, b, tc=128):
    B = p.shape[0]
    r0, n = rows
    tc = _pick_tile(width, tc)
    assert r0 % n == 0 and col0 % tc == 0
    rb, cb = r0 // n, col0 // tc
    return pl.pallas_call(
        _short_conv_body,
        out_shape=jax.ShapeDtypeStruct((B, n, width), F32),
        grid=(B, width // tc),
        in_specs=[pl.BlockSpec((None, n, tc), lambda b_, j: (b_, rb, cb + j)),
                  pl.BlockSpec((w.shape[0], tc), lambda b_, j: (0, j)),
                  pl.BlockSpec((1, tc), lambda b_, j: (0, j))],
        out_specs=pl.BlockSpec((None, n, tc), lambda b_, j: (b_, 0, j)),
        name="short_conv",
        compiler_params=pltpu.CompilerParams(dimension_semantics=("parallel", "parallel"),
                                             vmem_limit_bytes=_vmem_limit(4 * n * tc * 4 + 6 * n * tc * 4)),
    )(p, w, b.reshape(1, width))


def _rope_tables(L, Lc):
    rows = L // GRID_W
    row = jnp.repeat(jnp.arange(rows, dtype=F32), GRID_W)
    col = jnp.tile(jnp.arange(GRID_W, dtype=F32), rows)
    half = HEAD_DIM // 2
    inv = ROPE_THETA ** (-jnp.arange(0, half, 2, dtype=F32) / half)
    ang = jnp.concatenate([row[:, None] * inv, col[:, None] * inv], axis=-1)
    cos, sin = jnp.cos(ang), jnp.sin(ang)
    cos2 = jnp.repeat(cos, 2, axis=-1)
    sin2 = jnp.stack([-sin, sin], axis=-1).reshape(L, HEAD_DIM)
    cos2 = jnp.concatenate([cos2, jnp.ones((Lc, HEAD_DIM), F32)], axis=0)
    sin2 = jnp.concatenate([sin2, jnp.zeros((Lc, HEAD_DIM), F32)], axis=0)
    return cos2, sin2


def _dft_matrices(L):
    N = 2 * L
    k = jnp.arange(L, dtype=jnp.int32)
    n = jnp.arange(L, dtype=jnp.int32)
    if L % DFT_FINE == 0:
        n1 = jnp.arange(L // DFT_FINE, dtype=jnp.int32) * DFT_FINE
        n2 = jnp.arange(DFT_FINE, dtype=jnp.int32)
        a = (2.0 * math.pi / N) * ((k[:, None] * n1[None, :]) % N).astype(F32)
        b = (2.0 * math.pi / N) * ((k[:, None] * n2[None, :]) % N).astype(F32)
        ca, sa, cb, sb = jnp.cos(a)[:, :, None], jnp.sin(a)[:, :, None], jnp.cos(b)[:, None, :], jnp.sin(b)[:, None, :]
        cos = (ca * cb - sa * sb).reshape(L, L)
        sin = (sa * cb + ca * sb).reshape(L, L)
    else:
        ang = (2.0 * math.pi / N) * ((k[:, None] * n[None, :]) % N).astype(F32)
        cos, sin = jnp.cos(ang), jnp.sin(ang)
    nyq = jnp.where(n % 2 == 0, 1.0, -1.0).astype(F32)[None, :]
    first = (k == 0)[:, None]
    fwd = jnp.concatenate([cos, jnp.where(first, nyq, -sin)], axis=0)
    inv_re = jnp.where(first, 1.0, 2.0) * cos / N
    inv_im = jnp.where(first, nyq / N, -2.0 * sin / N)
    inv = jnp.concatenate([inv_re, inv_im], axis=0).T
    return fwd.astype(BF16), inv.astype(BF16)


def _pad_to(a, axis, size):
    pad = [(0, 0)] * a.ndim
    pad[axis] = (0, size - a.shape[axis])
    return jnp.pad(a, pad)


def _hyena_filters(L, w1, b1, w2, b2, w3, sin_freq, width):
    bands = (HY_EMB - 1) // 2
    t = jnp.linspace(0.0, 1.0, L, dtype=F32)[:, None]
    w = (2.0 * math.pi / L) * jnp.arange(L, dtype=F32)[:, None]
    f = jnp.linspace(1e-4, bands - 1, bands, dtype=F32)[None, :]
    z = jnp.concatenate([t, jnp.cos(f * w), -jnp.sin(f * w)], axis=-1)
    ffn = w1.shape[1]
    a = _matmul(_pad_to(z, 1, V7X_LANES), _pad_to(_pad_to(w1, 0, V7X_LANES), 1, V7X_LANES), name="hy_f1")
    a = jnp.sin(sin_freq[0] * (a[:, :ffn] + b1))
    a = _matmul(_pad_to(a, 1, V7X_LANES), _pad_to(_pad_to(w2, 0, V7X_LANES), 1, V7X_LANES), name="hy_f2")
    a = jnp.sin(sin_freq[1] * (a[:, :ffn] + b2))
    h = _matmul(_pad_to(a, 1, V7X_LANES), _pad_to(w3, 0, V7X_LANES), name="hy_f3")
    h = h.reshape(L, HY_ORDER, 2, width)
    rates = jnp.linspace(HY_SLOW_RATE, HY_FAST_RATE, width, dtype=F32)
    h = h * jnp.exp(-t[:, :, None, None] * rates)
    return h * lax.rsqrt(jnp.sum(jnp.square(h), axis=(0, 2), keepdims=True) + EPS)


def _hyena(u, filt, bias, dft):
    B, L, C3 = u.shape
    C = C3 // 3
    fwd, inv = dft
    first = (jnp.arange(L) == 0)[:, None]
    hf = filt[:, :, 0]
    hb = jnp.where(first[:, :, None], 0.0, filt[:, :, 1])
    hsum = jnp.moveaxis(hf + hb, 1, 0)
    hdif = jnp.moveaxis(hf - hb, 1, 0)
    sign = jnp.where(jnp.arange(L) % 2 == 0, 1.0, -1.0).astype(F32)[None, :, None]
    Kr_all = _matmul(fwd[:L], hsum.astype(BF16), name="hy_dft_filter")
    Ki_all = _matmul(fwd[L:], hdif.astype(BF16), name="hy_dft_filter")
    nyq = jnp.sum(hsum * sign, axis=1, keepdims=True)
    Ki_all = jnp.where(first, nyq, Ki_all)
    zb = u[..., :C].astype(BF16)
    z = u
    for o in range(HY_ORDER):
        yr, yi = _dft_fwd_mul(fwd, zb, Kr_all[o], Ki_all[o])
        last = o == HY_ORDER - 1
        res = _dft_inv_gate(inv, yr, yi, u, (1 + o) * C, z, 0, bias[o].reshape(1, C), emit_f32=not last)
        if last:
            return res[0]
        z, zb = res


def _dft_fwd_mul_body(fr_ref, fi_ref, z_ref, kr_ref, ki_ref, yr_ref, yi_ref):
    z = z_ref[...]
    ur = jnp.dot(fr_ref[...], z, preferred_element_type=F32)
    ui = jnp.dot(fi_ref[...], z, preferred_element_type=F32)
    kr, ki = kr_ref[...], ki_ref[...]
    tm = ur.shape[0]
    row = lax.broadcasted_iota(jnp.int32, (tm, 1), 0) + pl.program_id(0) * tm
    first = row == 0
    ii = ui * ki
    yr_ref[...] = (ur * kr - jnp.where(first, 0.0, ii)).astype(yr_ref.dtype)
    yi_ref[...] = jnp.where(first, ii, ur * ki + ui * kr).astype(yi_ref.dtype)


def _dft_fwd_mul(fwd, zb, kr, ki, tm=512, tn=512):
    B, L, C = zb.shape
    tm = _pick_tile(L, tm)
    tn = _pick_tile(C, tn)
    nlo = L // tm
    blk = 2 * tm * L * 2 + L * tn * 2 + 2 * tm * tn * 4 + 2 * tm * tn * 2
    out = jax.ShapeDtypeStruct((B, L, C), BF16)
    return pl.pallas_call(
        _dft_fwd_mul_body,
        out_shape=[out, out],
        grid=(nlo, B, C // tn),
        in_specs=[pl.BlockSpec((tm, L), lambda i, b, j: (i, 0)),
                  pl.BlockSpec((tm, L), lambda i, b, j: (nlo + i, 0)),
                  pl.BlockSpec((None, L, tn), lambda i, b, j: (b, 0, j)),
                  pl.BlockSpec((tm, tn), lambda i, b, j: (i, j)),
                  pl.BlockSpec((tm, tn), lambda i, b, j: (i, j))],
        out_specs=[pl.BlockSpec((None, tm, tn), lambda i, b, j: (b, i, j))] * 2,
        name="hy_dft_fwd",
        compiler_params=pltpu.CompilerParams(dimension_semantics=("parallel", "parallel", "parallel"),
                                             vmem_limit_bytes=_vmem_limit(2 * blk + 4 * tm * tn * 4)),
    )(fwd, fwd, zb, kr, ki)


def _dft_inv_gate_body(ir_ref, ii_ref, yr_ref, yi_ref, g_ref, z_ref, b_ref, *outs, emit_f32):
    conv = (jnp.dot(ir_ref[...], yr_ref[...], preferred_element_type=F32)
            + jnp.dot(ii_ref[...], yi_ref[...], preferred_element_type=F32))
    zn = g_ref[...] * (conv + b_ref[...] * z_ref[...])
    if emit_f32:
        outs[0][...] = zn
    outs[-1][...] = zn.astype(BF16)


def _dft_inv_gate(inv, yr, yi, u, gate_col, z, z_col, bias, emit_f32, tm=512, tn=512):
    B, L, C = yr.shape
    tm = _pick_tile(L, tm)
    tn = _pick_tile(C, tn)
    gj, zj = gate_col // tn, z_col // tn
    tile = pl.BlockSpec((None, tm, tn), lambda i, b, j: (b, i, j))
    spec = pl.BlockSpec((None, L, tn), lambda i, b, j: (b, 0, j))
    blk = 2 * tm * L * 2 + 2 * L * tn * 2 + 3 * tm * tn * 4 + tm * tn * 2
    out_shape = [jax.ShapeDtypeStruct((B, L, C), BF16)]
    if emit_f32:
        out_shape.insert(0, jax.ShapeDtypeStruct((B, L, C), F32))
    return pl.pallas_call(
        functools.partial(_dft_inv_gate_body, emit_f32=emit_f32),
        out_shape=out_shape,
        grid=(L // tm, B, C // tn),
        in_specs=[pl.BlockSpec((tm, L), lambda i, b, j: (i, 0)),
                  pl.BlockSpec((tm, L), lambda i, b, j: (i, 1)),
                  spec, spec,
                  pl.BlockSpec((None, tm, tn), lambda i, b, j: (b, i, gj + j)),
                  pl.BlockSpec((None, tm, tn), lambda i, b, j: (b, i, zj + j)),
                  pl.BlockSpec((1, tn), lambda i, b, j: (0, j))],
        out_specs=[tile] * len(out_shape),
        name="hy_dft_inv",
        compiler_params=pltpu.CompilerParams(dimension_semantics=("parallel", "parallel", "parallel"),
                                             vmem_limit_bytes=_vmem_limit(2 * blk + 2 * tm * tn * 4)),
    )(inv, inv, yr, yi, u, z, bias)


def kernel(x, c, ctx, c_ctx, ada_down, ada_up, ada_b, w_in, w_out, da_lambda, da_subln, gq_qk_norm,
           hy_conv_w, hy_conv_b, hy_f_w1, hy_f_b1, hy_f_w2, hy_f_b2, hy_f_w3, hy_sin_freq, hy_bias,
           ml_gate_b, ml_norm, ln_g, ln_b, router, ex_w1, ex_w3, ex_w2):
    B, L, D = x.shape
    Lc = ctx.shape[1]
    Lt = L + Lc
    depth = w_in.shape[0]
    E = router.shape[-1]
    alpha = (2 * depth) ** 0.25
    da_heads = D // (8 * HEAD_DIM)
    da_v = 2 * HEAD_DIM
    gq_heads = D // (4 * HEAD_DIM)
    gq_kv = gq_heads // 4
    gq_rep = gq_heads // gq_kv
    hy_w = D // 4
    ml_w = D - da_heads * da_v - gq_heads * HEAD_DIM - hy_w
    ml_d = ml_w // ML_HEADS
    widths = [da_heads * 2 * HEAD_DIM, da_heads * 2 * HEAD_DIM, da_heads * da_v, gq_heads * HEAD_DIM,
              gq_kv * HEAD_DIM, gq_kv * HEAD_DIM, hy_w, hy_w, hy_w, ml_w, ml_w, ml_w, ml_w, 4 * ML_HEADS]
    offs = np.concatenate([[0], np.cumsum(widths)]).astype(int)
    o_daq, o_dak, o_dav, o_gqq, o_gqk, o_gqv, o_hy, _, _, o_mlq, o_mlk, o_mlv, o_mlo, o_mlg, n_in = offs
    assert o_mlg % V7X_LANES == 0 and n_in - o_mlg == 4 * ML_HEADS
    att_w = o_hy
    S = min(256, Lc)
    assert L % S == 0 and Lc % S == 0 and L % Lc == 0

    qs = HEAD_DIM ** -0.5 * math.log2(math.e)
    n_da = 2 * da_heads
    plan = ([(None, True, qs)] * n_da + [(None, True, 1.0)] * n_da + [(None, False, 1.0)] * n_da
            + [(0, True, qs)] * gq_heads + [(1, True, 1.0)] * gq_kv + [(None, False, 1.0)] * gq_kv)
    cos2, sin2 = _rope_tables(L, Lc)
    dft_lat = _dft_matrices(L)
    dft_ctx = _dft_matrices(Lc)

    xs = jnp.concatenate([x, ctx], axis=1)
    cond = jnp.concatenate([c, c_ctx[None]], axis=0)
    cond = _pad_to(jax.nn.silu(cond), 0, V7X_SUBLANES)

    mods_all = []
    for layer in range(depth):
        m = _matmul(_matmul(cond, ada_down[layer], name="ada_down"), ada_up[layer], name="ada_up")
        mods_all.append((m[:B + 1] + ada_b[layer]).reshape(B + 1, N_MOD, D))

    for layer in range(depth):
        last = layer == depth - 1
        mods = mods_all[layer]

        if layer == 0:
            h = _modulate(xs, mods, L, 0, 1)
        h2 = h.reshape(B * Lt, D)
        p = _matmul(h2, w_in[layer][:, :o_mlg].astype(BF16), name="in_proj").reshape(B, Lt, o_mlg)
        w_gate = _pad_to(w_in[layer][:, o_mlg:], 1, V7X_LANES).astype(BF16)
        gate_pre = _matmul(h2, w_gate, name="in_proj_gates")[:, :4 * ML_HEADS]

        a = _qk_prep(p, cos2, sin2, gq_qk_norm[layer], plan)
        lam_init = 0.8 - 0.6 * math.exp(-0.3 * layer)
        lp = da_lambda[layer]
        lam = jnp.exp(jnp.sum(lp[0] * lp[1])) - jnp.exp(jnp.sum(lp[2] * lp[3])) + lam_init
        da = _attention(_da_attn_body, a, L, groups=da_heads, qw=da_v, kw=da_v, vw=da_v, ow=da_v,
                        q_col=o_daq // da_v, k_col=o_dak // da_v, v_col=o_dav // da_v, kv_div=1,
                        nchunk=DA_KEY_CHUNKS, scalars=(lam.reshape(1, 1),),
                        vectors=((da_subln[layer] * (1.0 - lam_init)).reshape(1, da_v),), name="da_attention")
        gq_w = GQ_HEADS_PER_STEP * HEAD_DIM
        gq = _attention(functools.partial(_gq_attn_body, hp=GQ_HEADS_PER_STEP), a, L,
                        groups=gq_heads // GQ_HEADS_PER_STEP, qw=gq_w, kw=HEAD_DIM, vw=HEAD_DIM, ow=gq_w,
                        q_col=o_gqq // gq_w, k_col=o_gqk // HEAD_DIM, v_col=o_gqv // HEAD_DIM,
                        kv_div=gq_rep // GQ_HEADS_PER_STEP, nchunk=GQ_KEY_CHUNKS, tq=GQ_QUERY_TILE,
                        name="gq_attention")

        conv_args = (hy_conv_w[layer], hy_conv_b[layer])
        filt_args = (hy_f_w1[layer], hy_f_b1[layer], hy_f_w2[layer], hy_f_b2[layer], hy_f_w3[layer],
                     hy_sin_freq[layer], hy_w)
        u_lat = _short_conv(p, o_hy, 3 * hy_w, (0, L), *conv_args)
        u_ctx = _short_conv(p, o_hy, 3 * hy_w, (L, Lc), *conv_args)
        hy_lat = _hyena(u_lat, _hyena_filters(L, *filt_args), hy_bias[layer], dft_lat)
        hy_ctx = _hyena(u_ctx, _hyena_filters(Lc, *filt_args), hy_bias[layer], dft_ctx)
        hy = jnp.concatenate([hy_lat, hy_ctx], axis=1)

        g = (gate_pre + ml_gate_b[layer]).reshape(B, Lt, 4, ML_HEADS)
        gates = jnp.transpose(g, (0, 3, 1, 2))
        hf, hb = _mlstm(p, gates, L, o_mlq // ml_d, o_mlk // ml_d, o_mlv // ml_d, ml_d, S)
        ml = _ml_post(hf, hb, p, o_mlo // ml_d, ml_norm[layer].reshape(ML_HEADS, 1, ml_d))

        slabs = [s.reshape(B * Lt, s.shape[-1]) for s in (da, gq, hy, ml)]
        mix = _matmul_multi(slabs, w_out[layer].astype(BF16), name="out_proj").reshape(B, Lt, D)

        xs, hm = _resid_ln(xs, mix, mods, mods, ln_g[layer, 0], ln_b[layer, 0], L, alpha, 2, 3, 4)

        logits = _matmul(hm.reshape(B * Lt, D), _pad_to(router[layer], 1, V7X_LANES), name="router")
        aff = jax.nn.softmax(logits[:, :E].reshape(B, Lt, E), axis=-1)
        cap, cap_c = CAPACITY_FACTOR * L // E, CAPACITY_FACTOR * Lc // E
        idx_l, gate_l = _route_unpack(_route(jnp.swapaxes(aff[:, :L], 1, 2), _route_tables(aff[:, :L]), cap), E, 0)
        idx_c, gate_c = _route_unpack(_route(jnp.swapaxes(aff[:, L:], 1, 2), _route_tables(aff[:, L:]), cap_c), E, L)
        idx = jnp.concatenate([idx_l, idx_c], axis=-1)
        gate = jnp.concatenate([gate_l, gate_c], axis=-1)
        R = cap + cap_c
        ffn = _expert_ffn(hm, jnp.zeros((B, Lt, D), F32), idx.reshape(B * E, 1, R), gate.reshape(B * E, R, 1),
                          ex_w1[layer].astype(BF16), ex_w3[layer].astype(BF16), ex_w2[layer].astype(BF16))

        if last:
            xs, _ = _resid_ln(xs, ffn, mods, mods, ln_g[layer, 1], ln_b[layer, 1], L, alpha, 5, rows=L)
        else:
            xs, h = _resid_ln(xs, ffn, mods, mods_all[layer + 1], ln_g[layer, 1], ln_b[layer, 1], L, alpha, 5, 0, 1,
                              h_dtype=BF16)
    return xs
```
